```python
import math
import jax
import jax.numpy as jnp
from jax import lax
import numpy as np

D_MODEL = 1024
BATCH = 1
SEQ = 16384
DEPTH = 2
DEC_BATCH = 32
DEC_SEQ = 8
PAST_LEN = 16384
PAGE_SIZE = 128

A_HEADS = 4
A_DK = 64
A_DV = 2 * A_DK
B_HEADS = 4
B_DK = 64
B_DV = 64
C_HEADS = 4
C_DH = 64
CMP_BLOCK = 64
SEL_BLOCK = CMP_BLOCK
TOP_N = 16
WINDOW = 512
MIX = A_HEADS * A_DV + B_HEADS * B_DV + C_HEADS * C_DH
HGRN_CHUNK = 64
Q_BLOCK = 128
ROPE_THETA = 10000.0
D_FF = 2816
N_EXPERTS = 8
TOP_K = 2
D_FF_EXPERT = 2816
N_DENSE = (DEPTH + 1) // 2
N_MOE = DEPTH // 2
RMS_EPS = 1e-6
NEG_INF = -1e30
FORCED_SCORE = 1e4
F_FLOOR = 1e-20
IN_SPLITS = (A_HEADS * 2 * A_DK, A_HEADS * 2 * A_DK, A_HEADS * A_DV,
             B_HEADS * B_DK, B_HEADS * B_DK, B_HEADS * B_DV, B_HEADS * B_DV,
             C_HEADS * C_DH, 2 * C_DH, 2 * C_DH, 2 * C_DH, C_HEADS * 3)
IN_COLS = sum(IN_SPLITS)

kernel_name = "hybrid_diffattn_hgrn2_nsa_decoder_step"

F32 = jnp.float32


def _rms(x, g):
    xf = x.astype(F32)
    y = xf * lax.rsqrt(jnp.mean(xf * xf, axis=-1, keepdims=True) + RMS_EPS)
    return (y * g.astype(F32)).astype(x.dtype)


def _rope(x, pos):
    d = x.shape[-1]
    inv = ROPE_THETA ** (-jnp.arange(0, d, 2, dtype=F32) / d)
    ang = pos.astype(F32)[:, None] * inv[None, :]
    cos = jnp.cos(ang)[None, :, None, :]
    sin = jnp.sin(ang)[None, :, None, :]
    xf = x.astype(F32)
    x1, x2 = xf[..., : d // 2], xf[..., d // 2:]
    return jnp.concatenate([x1 * cos - x2 * sin, x1 * sin + x2 * cos], axis=-1).astype(x.dtype)


def _masked_softmax(s, mask):
    s = jnp.where(mask, s.astype(F32), NEG_INF)
    m = jnp.max(s, axis=-1, keepdims=True)
    p = jnp.where(mask, jnp.exp(s - m), 0.0)
    return p / jnp.maximum(jnp.sum(p, axis=-1, keepdims=True), 1e-30)


def _over_query_blocks(fn, *qarrays):
    L = qarrays[0].shape[1]
    if L <= Q_BLOCK:
        return fn(*qarrays)
    nb = L // Q_BLOCK
    blocked = tuple(jnp.moveaxis(a.reshape(a.shape[0], nb, Q_BLOCK, *a.shape[2:]), 1, 0) for a in qarrays)
    out = lax.map(lambda args: fn(*args), blocked)
    return jnp.moveaxis(out, 0, 1).reshape(out.shape[1], L, *out.shape[3:])


def _blocks(a, nb, size):
    pad = nb * size - a.shape[1]
    a = jnp.pad(a, ((0, 0), (0, pad)) + ((0, 0),) * (a.ndim - 2))
    return a.reshape(a.shape[0], nb, size, *a.shape[2:])


def _diff_attention(q, k, v, qpos, kpos, lam):
    s = jnp.einsum("bqhmd,bkhmd->bhmqk", q, k).astype(F32) * (A_DK ** -0.5)
    mask = (kpos[None, None, :] <= qpos[:, :, None])[:, None, None]
    p = _masked_softmax(s, mask)
    a = p[:, :, 0] - lam * p[:, :, 1]
    return jnp.einsum("bhqk,bkhd->bqhd", a.astype(v.dtype), v)


def _hgrn2(q, k, v, logf, s0):
    bsz, L, H, _ = q.shape
    C = math.gcd(L, HGRN_CHUNK)
    n = L // C

    def chunks(a):
        a = a.astype(F32)
        return jnp.moveaxis(a.reshape(bsz, n, C, *a.shape[2:]), 1, 0)

    causal = jnp.tril(jnp.ones((C, C), dtype=bool))[None, :, :, None, None]

    def step(S, xs):
        qc, kc, vc, gc = xs
        G = jnp.cumsum(gc, axis=1)
        decay = jnp.exp(jnp.where(causal, G[:, :, None] - G[:, None, :], NEG_INF))
        att = jnp.einsum("bthk,bshk,btshk->bths", qc, kc, decay)
        o = jnp.einsum("bths,bshv->bthv", att, vc) + jnp.einsum("bthk,bhkv->bthv", qc * jnp.exp(G), S)
        g_last = G[:, -1]
        S = S * jnp.exp(g_last)[..., None] + jnp.einsum("bshk,bshv->bhkv", kc * jnp.exp(g_last[:, None] - G), vc)
        return S, o

    s_fin, o = lax.scan(step, s0.astype(F32), (chunks(q), chunks(k), chunks(v), chunks(logf)))
    o = jnp.moveaxis(o, 0, 1).reshape(bsz, L, H, v.shape[-1])
    return o.astype(v.dtype), s_fin.astype(s0.dtype)


def _nsa_compressed(q, kc, vc, w_cmp, qpos):
    nb = -(-kc.shape[1] // CMP_BLOCK)
    k_cmp = jnp.einsum("bnjd,jde->bne", _blocks(kc, nb, CMP_BLOCK), w_cmp[0])
    v_cmp = jnp.einsum("bnjd,jde->bne", _blocks(vc, nb, CMP_BLOCK), w_cmp[1])
    block_end = jnp.arange(nb) * CMP_BLOCK + (CMP_BLOCK - 1)
    mask = (block_end[None, None, :] <= qpos[:, :, None])[:, None]
    s = jnp.einsum("bqhd,bnd->bhqn", q, k_cmp).astype(F32) * (C_DH ** -0.5)
    p = _masked_softmax(s, mask)
    o = jnp.einsum("bhqn,bnd->bqhd", p.astype(v_cmp.dtype), v_cmp)
    return o, jnp.sum(p, axis=1)


def _nsa_select(importance, qpos):
    nb = importance.shape[-1]
    blk = jnp.arange(nb)[None, None, :]
    cur = (qpos // SEL_BLOCK)[:, :, None]
    score = jnp.where((blk == 0) | (blk == cur), FORCED_SCORE, jnp.where(blk < cur, importance, -1.0))
    return lax.top_k(score, min(TOP_N, nb))[1]


def _nsa_selected(q, idx, qpos, ksb, vsb):
    bsz, lq, H, _ = q.shape
    bi = jnp.arange(ksb.shape[0])[:, None, None]
    kg = ksb[bi, idx]
    vg = vsb[bi, idx]
    kpos = idx[..., None] * SEL_BLOCK + jnp.arange(SEL_BLOCK)
    mask = (kpos <= qpos[:, :, None, None]).reshape(kg.shape[0], lq, 1, -1)
    s = jnp.einsum("bqhd,bqnjd->bqhnj", q, kg).astype(F32) * (C_DH ** -0.5)
    p = _masked_softmax(s.reshape(s.shape[0], lq, H, -1), mask).reshape(s.shape)
    return jnp.einsum("bqhnj,bqnjd->bqhd", p.astype(vg.dtype), vg)


def _nsa_window_prompt(q, k, v):
    bsz, L, H, d = q.shape
    nq = L // Q_BLOCK
    nprev = -(-(WINDOW - 1) // Q_BLOCK)

    def band(a):
        ap = jnp.pad(a, ((0, 0), (nprev * Q_BLOCK, 0), (0, 0))).reshape(bsz, nq + nprev, Q_BLOCK, d)
        return jnp.concatenate([ap[:, j:j + nq] for j in range(nprev + 1)], axis=2)

    kb, vb = band(k), band(v)
    qb = q.reshape(bsz, nq, Q_BLOCK, H, d)
    qpos = jnp.arange(L).reshape(nq, Q_BLOCK)
    kpos = (jnp.arange(nq)[:, None] - nprev) * Q_BLOCK + jnp.arange((nprev + 1) * Q_BLOCK)[None, :]
    rel = qpos[:, :, None] - kpos[:, None, :]
    mask = (rel >= 0) & (rel < WINDOW) & (kpos[:, None, :] >= 0)
    s = jnp.einsum("bnqhd,bnkd->bnhqk", qb, kb).astype(F32) * (C_DH ** -0.5)
    p = _masked_softmax(s, mask[None, :, None])
    o = jnp.einsum("bnhqk,bnkd->bnqhd", p.astype(vb.dtype), vb)
    return o.reshape(bsz, L, H, d)


def _window_attend(q, k, v, qpos, kpos):
    rel = qpos[:, :, None] - kpos[None, None, :]
    mask = ((rel >= 0) & (rel < WINDOW))[:, None]
    s = jnp.einsum("bqhd,bkd->bhqk", q, k).astype(F32) * (C_DH ** -0.5)
    p = _masked_softmax(s, mask)
    return jnp.einsum("bhqk,bkd->bqhd", p.astype(v.dtype), v)


def _project(h, lw, pos):
    bsz, L, _ = h.shape
    offsets = np.cumsum(IN_SPLITS)[:-1].tolist()
    aq, ak, av, bq, bf, bv, bg, cq, ccmp, csel, cwin, cgate = jnp.split(h @ lw["w_in"], offsets, axis=-1)
    aq = _rope(aq.reshape(bsz, L, 2 * A_HEADS, A_DK), pos).reshape(bsz, L, A_HEADS, 2, A_DK)
    ak = _rope(ak.reshape(bsz, L, 2 * A_HEADS, A_DK), pos).reshape(bsz, L, A_HEADS, 2, A_DK)
    z = bf.reshape(bsz, L, B_HEADS, B_DK).astype(F32)
    lb = lw["lb"].astype(F32).reshape(B_HEADS, B_DK)
    f = lb + (1.0 - lb) * jax.nn.sigmoid(z)
    logf = jnp.log(jnp.maximum(f, F_FLOOR))
    cq = cq.reshape(bsz, L, C_HEADS, C_DH)

    def key_rot(kv):
        kv = kv.reshape(bsz, L, 2, C_DH)
        return jnp.concatenate([_rope(kv[:, :, :1], pos), kv[:, :, 1:]], axis=2)

    return dict(
        aq=aq, ak=ak, av=av.reshape(bsz, L, A_HEADS, A_DV),
        bq=jax.nn.silu(bq.reshape(bsz, L, B_HEADS, B_DK).astype(F32)), bk=1.0 - f,
        bv=bv.reshape(bsz, L, B_HEADS, B_DV), blogf=logf, bg=bg.reshape(bsz, L, B_HEADS, B_DV),
        cq=cq, cq_rot=_rope(cq, pos), cmp=ccmp.reshape(bsz, L, 2, C_DH),
        sel=key_rot(csel), win=key_rot(cwin), cgate=cgate.reshape(bsz, L, C_HEADS, 3))


def _merge(t, a_o, b_o, o_cmp, o_sel, o_win, lw):
    bsz, L = a_o.shape[:2]
    a = _rms(a_o, lw["diff_subln"]) * (1.0 - lw["lam_init"])
    b = _rms(b_o, lw["hgrn_norm"]) * jax.nn.silu(t["bg"])
    g = jax.nn.sigmoid(t["cgate"].astype(F32)).astype(o_cmp.dtype)
    c = g[..., 0:1] * o_cmp + g[..., 1:2] * o_sel + g[..., 2:3] * o_win
    cat = jnp.concatenate([a.reshape(bsz, L, -1), b.reshape(bsz, L, -1).astype(a.dtype),
                           c.reshape(bsz, L, -1).astype(a.dtype)], axis=-1)
    return cat @ lw["w_out"]


def _mix_prompt(h, lw):
    bsz, L, _ = h.shape
    pos = jnp.arange(L)
    qpos = pos[None]
    t = _project(h, lw, pos)
    a_o = _over_query_blocks(lambda q, qp: _diff_attention(q, t["ak"], t["av"], qp, pos, lw["lam"]), t["aq"], qpos)
    s0 = jnp.zeros((bsz, B_HEADS, B_DK, B_DV), h.dtype)
    b_o, s_fin = _hgrn2(t["bq"], t["bk"], t["bv"], t["blogf"], s0)
    o_cmp, imp = _nsa_compressed(t["cq"], t["cmp"][:, :, 0], t["cmp"][:, :, 1], lw["w_cmp"], qpos)
    idx = _nsa_select(imp, qpos)
    nb = imp.shape[-1]
    ksb = _blocks(t["sel"][:, :, 0], nb, SEL_BLOCK)
    vsb = _blocks(t["sel"][:, :, 1], nb, SEL_BLOCK)
    o_sel = _over_query_blocks(lambda q, i, qp: _nsa_selected(q, i, qp, ksb, vsb), t["cq_rot"], idx, qpos)
    o_win = _nsa_window_prompt(t["cq_rot"], t["win"][:, :, 0], t["win"][:, :, 1])
    y = _merge(t, a_o, b_o, o_cmp, o_sel, o_win, lw)
    W = min(WINDOW, L)
    return y, (t["ak"].reshape(bsz, L, A_HEADS, 2 * A_DK), t["av"], t["cmp"], t["sel"], t["win"][:, L - W:], s_fin)


def _mix_sample(h, lw, pool_k, pool_v, pool_cmp, pool_sel, win_buf, s0, page_table):
    bsz, L, _ = h.shape
    pos = PAST_LEN + jnp.arange(L)
    qpos = pos[None]
    t = _project(h, lw, pos)

    def with_past(pool, new):
        past = pool[page_table].reshape(bsz, -1, *new.shape[2:])
        return jnp.concatenate([past, new.astype(past.dtype)], axis=1)

    k_new = t["ak"].reshape(bsz, L, A_HEADS, 2 * A_DK)
    k_all = with_past(pool_k, k_new).reshape(bsz, -1, A_HEADS, 2, A_DK)
    v_all = with_past(pool_v, t["av"])
    kpos = jnp.arange(k_all.shape[1])
    a_o = _over_query_blocks(lambda q, qp: _diff_attention(q, k_all, v_all, qp, kpos, lw["lam"]), t["aq"], qpos)
    b_o, s_new = _hgrn2(t["bq"], t["bk"], t["bv"], t["blogf"], s0)
    cmp_all = with_past(pool_cmp, t["cmp"])
    sel_all = with_past(pool_sel, t["sel"])
    o_cmp, imp = _nsa_compressed(t["cq"], cmp_all[:, :, 0], cmp_all[:, :, 1], lw["w_cmp"], qpos)
    idx = _nsa_select(imp, qpos)
    nb = imp.shape[-1]
    ksb = _blocks(sel_all[:, :, 0], nb, SEL_BLOCK)
    vsb = _blocks(sel_all[:, :, 1], nb, SEL_BLOCK)
    o_sel = _over_query_blocks(lambda q, i, qp: _nsa_selected(q, i, qp, ksb, vsb), t["cq_rot"], idx, qpos)
    W = win_buf.shape[1]
    win_all = jnp.concatenate([win_buf, t["win"].astype(win_buf.dtype)], axis=1)
    wpos = PAST_LEN - W + jnp.arange(W + L)
    o_win = _window_attend(t["cq_rot"], win_all[:, :, 0], win_all[:, :, 1], qpos, wpos)
    y = _merge(t, a_o, b_o, o_cmp, o_sel, o_win, lw)
    return y, (k_new, t["av"], t["cmp"], t["sel"], win_all[:, L:], s_new)


def _swiglu(h, w_gu, w_down):
    g, u = jnp.split(h @ w_gu, 2, axis=-1)
    return (jax.nn.silu(g) * u) @ w_down


def _moe(h, w_router, b_router, w_gu, w_down):
    logits = (h @ w_router + b_router).astype(F32)
    top_val, top_idx = lax.top_k(logits, TOP_K)
    gate = jax.nn.softmax(top_val, axis=-1)
    comb = jnp.sum(jax.nn.one_hot(top_idx, N_EXPERTS, dtype=F32) * gate[..., None], axis=-2).astype(h.dtype)
    out = jnp.zeros_like(h)
    for e in range(N_EXPERTS):
        out = out + comb[..., e:e + 1] * _swiglu(h, w_gu[e], w_down[e])
    return out


def _block(x, c, lw, mix_fn, ffn_fn):
    mod = jax.nn.silu(c) @ lw["w_ada"] + lw["b_ada"]
    sh1, sc1, g1, sh2, sc2, g2 = [m[:, None, :] for m in jnp.split(mod, 6, axis=-1)]
    h = _rms(x, lw["norm_mix"]) * (1.0 + sc1) + sh1
    y, states = mix_fn(h)
    x = x + g1 * y
    h = _rms(x, lw["norm_ffn"]) * (1.0 + sc2) + sh2
    x = x + g2 * ffn_fn(h)
    return x, states


def setup_inputs(seed: int = 0) -> dict:
    key = jax.random.key(seed)
    ks = list(jax.random.split(key, 40))
    it = iter(ks)

    def nrm(shape, scale=1.0):
        return jax.random.normal(next(it), shape, F32) * scale

    n_pages = PAST_LEN // PAGE_SIZE
    used = DEC_BATCH * n_pages
    n_pool = used + max(1, used // 4)
    page_table = jax.random.permutation(next(it), n_pool)[:used].reshape(DEC_BATCH, n_pages).astype(jnp.int32)
    win_s = min(WINDOW, PAST_LEN)
    return {
        "x_prompt": nrm((BATCH, SEQ, D_MODEL)),
        "x_sample": nrm((DEC_BATCH, DEC_SEQ, D_MODEL)),
        "cache_diff_k": nrm((DEPTH, n_pool, PAGE_SIZE, A_HEADS, 2 * A_DK)),
        "cache_diff_v": nrm((DEPTH, n_pool, PAGE_SIZE, A_HEADS, A_DV)),
        "cache_nsa_cmp": nrm((DEPTH, n_pool, PAGE_SIZE, 2, C_DH)),
        "cache_nsa_sel": nrm((DEPTH, n_pool, PAGE_SIZE, 2, C_DH)),
        "state_nsa_win": nrm((DEPTH, DEC_BATCH, win_s, 2, C_DH)),
        "state_hgrn": nrm((DEPTH, DEC_BATCH, B_HEADS, B_DK, B_DV), 0.5),
        "page_table": page_table,
        "c_prompt": nrm((BATCH, D_MODEL)),
        "c_sample": nrm((DEC_BATCH, D_MODEL)),
        "w_in": nrm((DEPTH, D_MODEL, IN_COLS), D_MODEL ** -0.5),
        "w_out": nrm((DEPTH, MIX, D_MODEL), MIX ** -0.5),
        "w_cmp": nrm((DEPTH, 2, CMP_BLOCK, C_DH, C_DH), (CMP_BLOCK * C_DH) ** -0.5),
        "diff_lambda": nrm((DEPTH, 4, A_DK), 0.1),
        "diff_subln": 1.0 + nrm((DEPTH, A_DV), 0.02),
        "hgrn_lb_logits": nrm((DEPTH, B_HEADS * B_DK), 0.5),
        "hgrn_norm": 1.0 + nrm((DEPTH, B_DV), 0.02),
        "norm_mix": 1.0 + nrm((DEPTH, D_MODEL), 0.02),
        "norm_ffn": 1.0 + nrm((DEPTH, D_MODEL), 0.02),
        "norm_final": 1.0 + nrm((D_MODEL,), 0.02),
        "w_ada": nrm((DEPTH, D_MODEL, 6 * D_MODEL), 0.5 * D_MODEL ** -0.5),
        "b_ada": nrm((DEPTH, 6 * D_MODEL), 0.02),
        "ffn_w_gate_up": nrm((N_DENSE, D_MODEL, 2 * D_FF), D_MODEL ** -0.5),
        "ffn_w_down": nrm((N_DENSE, D_FF, D_MODEL), D_FF ** -0.5),
        "moe_w_router": nrm((N_MOE, D_MODEL, N_EXPERTS), D_MODEL ** -0.5),
        "moe_b_router": nrm((N_MOE, N_EXPERTS), 0.01),
        "moe_w_gate_up": nrm((N_MOE, N_EXPERTS, D_MODEL, 2 * D_FF_EXPERT), D_MODEL ** -0.5),
        "moe_w_down": nrm((N_MOE, N_EXPERTS, D_FF_EXPERT, D_MODEL), D_FF_EXPERT ** -0.5),
    }


def reference(x_prompt, x_sample, cache_diff_k, cache_diff_v, cache_nsa_cmp, cache_nsa_sel, state_nsa_win,
              state_hgrn, page_table, c_prompt, c_sample, w_in, w_out, w_cmp, diff_lambda, diff_subln,
              hgrn_lb_logits, hgrn_norm, norm_mix, norm_ffn, norm_final, w_ada, b_ada, ffn_w_gate_up,
              ffn_w_down, moe_w_router, moe_b_router, moe_w_gate_up, moe_w_down):
    lb_sm = jax.nn.softmax(hgrn_lb_logits.astype(F32), axis=0)
    lower_bounds = jnp.cumsum(lb_sm, axis=0) - lb_sm[0]
    xp, xs = x_prompt, x_sample
    st_p = [[] for _ in range(6)]
    st_s = [[] for _ in range(6)]
    for l in range(DEPTH):
        lam_init = 0.8 - 0.6 * math.exp(-0.3 * l)
        dl = diff_lambda[l].astype(F32)
        lam = jnp.exp(jnp.sum(dl[0] * dl[1])) - jnp.exp(jnp.sum(dl[2] * dl[3])) + lam_init
        lw = dict(w_in=w_in[l], w_out=w_out[l], w_cmp=w_cmp[l], lam=lam, lam_init=lam_init,
                  diff_subln=diff_subln[l], lb=lower_bounds[l], hgrn_norm=hgrn_norm[l],
                  norm_mix=norm_mix[l], norm_ffn=norm_ffn[l], w_ada=w_ada[l], b_ada=b_ada[l])
        j = l // 2
        if l % 2 == 0:
            ffn_fn = lambda h: _swiglu(h, ffn_w_gate_up[j], ffn_w_down[j])
        else:
            ffn_fn = lambda h: _moe(h, moe_w_router[j], moe_b_router[j], moe_w_gate_up[j], moe_w_down[j])
        xp, sp = _block(xp, c_prompt, lw, lambda h: _mix_prompt(h, lw), ffn_fn)
        xs, ss = _block(xs, c_sample, lw, lambda h: _mix_sample(h, lw, cache_diff_k[l], cache_diff_v[l],
                                                                 cache_nsa_cmp[l], cache_nsa_sel[l],
                                                                 state_nsa_win[l], state_hgrn[l], page_table),
                        ffn_fn)
        for i in range(6):
            st_p[i].append(sp[i])
            st_s[i].append(ss[i])
    y_prompt = _rms(xp, norm_final)
    y_sample = _rms(xs, norm_final)
    p_diff_k, p_diff_v, p_nsa_cmp, p_nsa_sel, p_nsa_win, p_hgrn = [jnp.stack(a, axis=0) for a in st_p]
    s_diff_k, s_diff_v, s_nsa_cmp, s_nsa_sel, s_nsa_win, s_hgrn = [jnp.stack(a, axis=0) for a in st_s]
    return (y_prompt, y_sample, p_diff_k, p_diff_v, p_nsa_cmp, p_nsa_sel, p_nsa_win, p_hgrn,
            s_diff_k, s_diff_v, s_nsa_cmp, s_nsa_sel, s_nsa_win, s_hgrn)
```

```python
import functools
import math

import numpy as np
import jax
import jax.numpy as jnp
from jax import lax
from jax.experimental import pallas as pl
from jax.experimental.pallas import tpu as pltpu

F32 = jnp.float32
BF16 = jnp.bfloat16

A_HEADS = 4
A_DK = 64
A_DV = 128
B_HEADS = 4
B_DK = 64
B_DV = 64
C_HEADS = 4
C_DH = 64
CMP_BLOCK = 64
TOP_N = 16
WINDOW = 512
ROPE_THETA = 10000.0
N_EXPERTS = 8
RMS_EPS = 1e-6
NEG_INF = -1e30
FORCED_SCORE = 1e4
F_FLOOR = 1e-20
QK_SCALE = 0.125
LANES = 128
MOE_SUB = 128

_SPL = (512, 512, 512, 256, 256, 256, 256, 256, 128, 128, 128, 12)
_OFF = tuple(int(v) for v in np.cumsum((0,) + _SPL))
IN_COLS = _OFF[-1]


def _cp(sem, vmem_mb=48):
    return pltpu.CompilerParams(dimension_semantics=sem, vmem_limit_bytes=vmem_mb * 1024 * 1024)


def _silu(x):
    return x * jax.nn.sigmoid(x)


def _nt(a, b):
    return lax.dot_general(a, b, (((1,), (1,)), ((), ())), preferred_element_type=F32)


def _tn(a, b):
    return lax.dot_general(a, b, (((0,), (0,)), ((), ())), preferred_element_type=F32)


def _mm(a, b):
    return jnp.dot(a, b, preferred_element_type=F32)


def _lane(shape):
    return lax.broadcasted_iota(jnp.int32, shape, 1)


def _row(shape):
    return lax.broadcasted_iota(jnp.int32, shape, 0)


def _rms_mod(x, nw, sc, sh):
    h = x * lax.rsqrt(jnp.mean(x * x, axis=-1, keepdims=True) + RMS_EPS) * nw
    return h * (1.0 + sc) + sh


def _ada_kernel(c_ref, w_ref, b_ref, o_ref):
    s = _silu(c_ref[...]).astype(BF16)
    o_ref[0] = _mm(s, w_ref[0].astype(BF16)) + b_ref[0]


def _ada(c_all, w_ada, b_ada):
    depth, d, n = w_ada.shape
    r = c_all.shape[0]
    tn = 1536
    return pl.pallas_call(
        _ada_kernel,
        grid=(depth, n // tn),
        in_specs=[pl.BlockSpec((r, d), lambda l, j: (0, 0)),
                  pl.BlockSpec((1, d, tn), lambda l, j: (l, 0, j)),
                  pl.BlockSpec((1, 1, tn), lambda l, j: (l, 0, j))],
        out_specs=pl.BlockSpec((1, r, tn), lambda l, j: (l, 0, j)),
        out_shape=jax.ShapeDtypeStruct((depth, r, n), F32),
        compiler_params=_cp(("arbitrary", "arbitrary")),
        name="ada_mod",
    )(c_all, w_ada, b_ada.reshape(depth, 1, n))


def _proj_kernel(layer, x_ref, sc_ref, sh_ref, nw_ref, w_ref, cos_ref, sin_ref, lbl_ref,
                 aq_ref, ak_ref, av_ref, hq_ref, hk_ref, hf_ref, hv_ref, hg_ref,
                 cq_ref, cqr_ref, cmp_ref, sel_ref, win_ref, cg_ref):
    hb = _rms_mod(x_ref[...], nw_ref[...], sc_ref[...], sh_ref[...]).astype(BF16)
    cos = cos_ref[...]
    sin = sin_ref[...]
    lane = _lane(cos.shape)
    first = (lane % 64) < 32

    def mm(i):
        return _mm(hb, w_ref[:, _OFF[i]:_OFF[i + 1]])

    def rope(y, c, s):
        rot = jnp.where(first, pltpu.roll(y, 96, 1), pltpu.roll(y, 32, 1))
        return y * c + rot * s

    y = mm(0)
    for g in range(4):
        aq_ref[:, 128 * g:128 * (g + 1)] = (rope(y[:, 128 * g:128 * (g + 1)], cos, sin) * QK_SCALE).astype(BF16)
    y = mm(1)
    for g in range(4):
        ak_ref[:, 128 * g:128 * (g + 1)] = rope(y[:, 128 * g:128 * (g + 1)], cos, sin)
    av_ref[...] = mm(2)
    hq_ref[...] = _silu(mm(3))
    lbl = lbl_ref[...]
    e = jnp.exp(lbl - jnp.max(lbl, axis=0, keepdims=True))
    sm = e / jnp.sum(e, axis=0, keepdims=True)
    lb = jnp.sum(sm[0:layer + 1], axis=0, keepdims=True) - sm[0:1]
    f = lb + (1.0 - lb) * jax.nn.sigmoid(mm(4))
    hf_ref[...] = jnp.log(jnp.maximum(f, F_FLOOR))
    hk_ref[...] = 1.0 - f
    hv_ref[...] = mm(5)
    hg_ref[...] = mm(6)
    y = mm(7)
    cq_ref[...] = (y * QK_SCALE).astype(BF16)
    for g in range(2):
        cqr_ref[:, 128 * g:128 * (g + 1)] = (rope(y[:, 128 * g:128 * (g + 1)], cos, sin) * QK_SCALE).astype(BF16)
    cmp_ref[...] = mm(8)
    cos_k = jnp.where(lane < 64, cos, 1.0)
    sin_k = jnp.where(lane < 64, sin, 0.0)
    sel_ref[...] = rope(mm(9), cos_k, sin_k)
    win_ref[...] = rope(mm(10), cos_k, sin_k)
    cg_ref[...] = mm(11)


def _proj(layer, x, sc, sh, nw, w_bf, cos, sin, lbl, tm):
    rows, d = x.shape
    mrows = sc.shape[0]
    mod_spec = (pl.BlockSpec((1, d), lambda i: (0, 0)) if mrows == 1
                else pl.BlockSpec((tm, d), lambda i: (i, 0)))
    widths = (512, 512, 512, 256, 256, 256, 256, 256, 256, 256, 128, 128, 128, 12)
    dtypes = (BF16, F32, F32, F32, F32, F32, F32, F32, BF16, BF16, F32, F32, F32, F32)
    return pl.pallas_call(
        functools.partial(_proj_kernel, layer),
        grid=(rows // tm,),
        in_specs=[pl.BlockSpec((tm, d), lambda i: (i, 0)), mod_spec, mod_spec,
                  pl.BlockSpec((1, d), lambda i: (0, 0)),
                  pl.BlockSpec((d, IN_COLS), lambda i: (0, 0)),
                  pl.BlockSpec((tm, LANES), lambda i: (i, 0)),
                  pl.BlockSpec((tm, LANES), lambda i: (i, 0)),
                  pl.BlockSpec(lbl.shape, lambda i: (0, 0))],
        out_specs=[pl.BlockSpec((tm, w), lambda i: (i, 0)) for w in widths],
        out_shape=[jax.ShapeDtypeStruct((rows, w), dt) for w, dt in zip(widths, dtypes)],
        compiler_params=_cp(("arbitrary",), 56),
        name="in_proj",
    )(x, sc, sh, nw, w_bf, cos, sin, lbl)


def _online(s, m_prev, l_prev):
    m_next = jnp.maximum(m_prev, jnp.max(s, axis=1, keepdims=True))
    p = jnp.exp(s - m_next)
    alpha = jnp.exp(m_prev - m_next)
    l_next = alpha * l_prev + jnp.sum(p, axis=1, keepdims=True)
    return p, alpha, m_next, l_next


def _diff_lambda(dl, lam_init):
    a = jnp.sum(dl[0:1] * dl[1:2], axis=1, keepdims=True)
    b = jnp.sum(dl[2:3] * dl[3:4], axis=1, keepdims=True)
    return jnp.exp(a) - jnp.exp(b) + lam_init


def _causal_pairs(nq, tq, tk):
    ii, jj = [], []
    for i in range(nq):
        for j in range(((i + 1) * tq - 1) // tk + 1):
            ii.append(i)
            jj.append(j)
    return jnp.asarray(ii, jnp.int32), jnp.asarray(jj, jnp.int32)


def _diff_flash_kernel(tq, tk, lam_init, ii_ref, jj_ref, q_ref, k_ref, v_ref, dl_ref, o_ref,
                       m_sc, l_sc, acc_sc):
    s_idx = pl.program_id(1)
    i = ii_ref[s_idx]
    j = jj_ref[s_idx]

    @pl.when(j == 0)
    def _():
        m_sc[...] = jnp.full(m_sc.shape, NEG_INF, F32)
        l_sc[...] = jnp.zeros(l_sc.shape, F32)
        acc_sc[...] = jnp.zeros(acc_sc.shape, F32)

    q = q_ref[...]
    kb = k_ref[...].astype(BF16)
    vb = v_ref[...].astype(BF16)
    lane = _lane(q.shape)
    qs = (jnp.where(lane < 64, q, jnp.zeros_like(q)), jnp.where(lane >= 64, q, jnp.zeros_like(q)))

    def step(masked):
        if masked:
            valid = (j * tk + _lane((tq, tk))) <= (i * tq + _row((tq, tk)))
        for mi in range(2):
            s = _nt(qs[mi], kb)
            if masked:
                s = jnp.where(valid, s, NEG_INF)
            p, alpha, m_next, l_next = _online(s, m_sc[mi][:, :1], l_sc[mi][:, :1])
            acc_sc[mi] = acc_sc[mi] * alpha + _mm(p.astype(BF16), vb)
            m_sc[mi] = jnp.broadcast_to(m_next, (tq, LANES))
            l_sc[mi] = jnp.broadcast_to(l_next, (tq, LANES))

    needs_mask = (j * tk + tk - 1) > (i * tq)
    pl.when(needs_mask)(lambda: step(True))
    pl.when(jnp.logical_not(needs_mask))(lambda: step(False))

    @pl.when(j == ((i + 1) * tq - 1) // tk)
    def _():
        lam = _diff_lambda(dl_ref[...], lam_init)
        o_ref[...] = acc_sc[0] / l_sc[0] - lam * (acc_sc[1] / l_sc[1])


def _diff_flash(aq, ak, av, dl, lam_init, tq, tk):
    L = aq.shape[0]
    nq = L // tq
    ii, jj = _causal_pairs(nq, tq, tk)
    grid_spec = pltpu.PrefetchScalarGridSpec(
        num_scalar_prefetch=2,
        grid=(A_HEADS, int(ii.shape[0])),
        in_specs=[pl.BlockSpec((tq, 128), lambda h, s, ii, jj: (ii[s], h)),
                  pl.BlockSpec((tk, 128), lambda h, s, ii, jj: (jj[s], h)),
                  pl.BlockSpec((tk, 128), lambda h, s, ii, jj: (jj[s], h)),
                  pl.BlockSpec(dl.shape, lambda h, s, ii, jj: (0, 0))],
        out_specs=pl.BlockSpec((tq, 128), lambda h, s, ii, jj: (ii[s], h)),
        scratch_shapes=[pltpu.VMEM((2, tq, LANES), F32), pltpu.VMEM((2, tq, LANES), F32),
                        pltpu.VMEM((2, tq, 128), F32)])
    return pl.pallas_call(
        functools.partial(_diff_flash_kernel, tq, tk, lam_init),
        grid_spec=grid_spec,
        out_shape=jax.ShapeDtypeStruct((L, 512), F32),
        compiler_params=_cp(("arbitrary", "arbitrary")),
        name="diff_flash",
    )(ii, jj, aq, ak, av, dl)


def _kk(kv):
    return jnp.where(_lane(kv.shape) < 64, kv, pltpu.roll(kv, 64, 1)).astype(BF16)


def _head_q(q_ref_or_val, h):
    qp = q_ref_or_val[:, 128 * (h // 2):128 * (h // 2 + 1)]
    lane = _lane(qp.shape)
    keep = (lane < 64) if h % 2 == 0 else (lane >= 64)
    return jnp.where(keep, qp, jnp.zeros_like(qp))


def _head_qb(q_ref_or_val, h):
    return _head_q(q_ref_or_val, h).astype(BF16)


def _pair_out(o_even, o_odd):
    return jnp.where(_lane(o_even.shape) < 64, pltpu.roll(o_even, 64, 1), o_odd)


def _stack_low_q(q):
    parts = []
    for g in range(C_HEADS // 2):
        qp = q[:, 128 * g:128 * (g + 1)]
        low = _lane(qp.shape) < 64
        parts += [jnp.where(low, qp, 0.0), jnp.where(low, pltpu.roll(qp, 64, 1), 0.0)]
    return jnp.concatenate(parts, axis=0).astype(BF16)


def _feature_major(a):
    n = a.ndim
    t = jnp.transpose(a, tuple(range(n - 3)) + (n - 2, n - 1, n - 3))
    return t.reshape(a.shape[:-3] + (2 * C_DH, a.shape[-3]))


def _sel_flash_kernel(tq, tk, ii_ref, jj_ref, q_ref, kv_ref, msk_ref, o_ref, m_sc, l_sc, acc_sc):
    s_idx = pl.program_id(0)
    i = ii_ref[s_idx]
    j = jj_ref[s_idx]
    bpt = tk // CMP_BLOCK

    @pl.when(j == 0)
    def _():
        m_sc[...] = jnp.full(m_sc.shape, NEG_INF, F32)
        l_sc[...] = jnp.zeros(l_sc.shape, F32)
        acc_sc[...] = jnp.zeros(acc_sc.shape, F32)

    kv = kv_ref[...]
    kvb = kv.astype(BF16)
    kk = _kk(kv)
    nbl = msk_ref.shape[1]
    boff = (j * bpt) % nbl
    expand = jnp.where(_row((nbl, tk)) == boff + _lane((nbl, tk)) // CMP_BLOCK, 1.0, 0.0).astype(BF16)
    sel = _mm(msk_ref[...].astype(BF16), expand) > 0.5
    valid = jnp.logical_and(sel, (j * tk + _lane((tq, tk))) <= (i * tq + _row((tq, tk))))
    for h in range(C_HEADS):
        s = jnp.where(valid, _nt(_head_qb(q_ref, h), kk), NEG_INF)
        p, alpha, m_next, l_next = _online(s, m_sc[h][:, :1], l_sc[h][:, :1])
        acc_sc[h] = acc_sc[h] * alpha + _mm(p.astype(BF16), kvb)
        m_sc[h] = jnp.broadcast_to(m_next, (tq, LANES))
        l_sc[h] = jnp.broadcast_to(l_next, (tq, LANES))

    @pl.when(j == ((i + 1) * tq - 1) // tk)
    def _():
        for g in range(2):
            o_ref[:, 128 * g:128 * (g + 1)] = _pair_out(acc_sc[2 * g] / l_sc[2 * g],
                                                        acc_sc[2 * g + 1] / l_sc[2 * g + 1])


def _sel_flash(cqr, sel, selmask, tq, tk):
    L = cqr.shape[0]
    nb = selmask.shape[1]
    nbl = min(nb, LANES)
    bpt = tk // CMP_BLOCK
    ii, jj = _causal_pairs(L // tq, tq, tk)
    grid_spec = pltpu.PrefetchScalarGridSpec(
        num_scalar_prefetch=2,
        grid=(int(ii.shape[0]),),
        in_specs=[pl.BlockSpec((tq, 256), lambda s, ii, jj: (ii[s], 0)),
                  pl.BlockSpec((tk, 128), lambda s, ii, jj: (jj[s], 0)),
                  pl.BlockSpec((tq, nbl), lambda s, ii, jj: (ii[s], (jj[s] * bpt) // nbl))],
        out_specs=pl.BlockSpec((tq, 256), lambda s, ii, jj: (ii[s], 0)),
        scratch_shapes=[pltpu.VMEM((4, tq, LANES), F32), pltpu.VMEM((4, tq, LANES), F32),
                        pltpu.VMEM((4, tq, 128), F32)])
    return pl.pallas_call(
        functools.partial(_sel_flash_kernel, tq, tk),
        grid_spec=grid_spec,
        out_shape=jax.ShapeDtypeStruct((L, 256), F32),
        compiler_params=_cp(("arbitrary",)),
        name="sel_flash",
    )(ii, jj, cqr, sel, selmask)


def _win_prompt_kernel(t, q_ref, kp_ref, kc_ref, o_ref):
    i = pl.program_id(0)
    kvp = kp_ref[...]
    kvc = kc_ref[...]
    kkp, kkc = _kk(kvp), _kk(kvc)
    r, c = _row((t, t)), _lane((t, t))
    vp = jnp.logical_and(c > r, i > 0)
    vc = c <= r
    outs = []
    for h in range(C_HEADS):
        qh = _head_qb(q_ref, h)
        sp =jnp.where(vp, _nt(qh, kkp), NEG_INF)
        sc = jnp.where(vc, _nt(qh, kkc), NEG_INF)
        m = jnp.maximum(jnp.max(sp, axis=1, keepdims=True), jnp.max(sc, axis=1, keepdims=True))
        pp = jnp.exp(sp - m)
        pc = jnp.exp(sc - m)
        l = jnp.sum(pp, axis=1, keepdims=True) + jnp.sum(pc, axis=1, keepdims=True)
        outs.append((_mm(pp.astype(BF16), kvp.astype(BF16)) + _mm(pc.astype(BF16), kvc.astype(BF16))) / l)
    for g in range(2):
        o_ref[:, 128 * g:128 * (g + 1)] = _pair_out(outs[2 * g], outs[2 * g + 1])


def _win_prompt(cqr, win, t):
    L = cqr.shape[0]
    return pl.pallas_call(
        functools.partial(_win_prompt_kernel, t),
        grid=(L // t,),
        in_specs=[pl.BlockSpec((t, 256), lambda i: (i, 0)),
                  pl.BlockSpec((t, 128), lambda i: (jnp.maximum(i - 1, 0), 0)),
                  pl.BlockSpec((t, 128), lambda i: (i, 0))],
        out_specs=pl.BlockSpec((t, 256), lambda i: (i, 0)),
        out_shape=jax.ShapeDtypeStruct((L, 256), F32),
        compiler_params=_cp(("arbitrary",)),
        name="win_prompt",
    )(cqr, win, win)


def _cmp_kernel(tq, qpos0, topn, q_ref, kv_ref, o_ref, msk_ref):
    t = pl.program_id(1)
    kv = kv_ref[0]
    nb = kv.shape[0]
    kvb = kv.astype(BF16)
    kk = _kk(kv)
    blk = _lane((tq, nb))
    qpos = qpos0 + t * tq + _row((tq, nb))
    valid = (blk * CMP_BLOCK + (CMP_BLOCK - 1)) <= qpos
    imp = jnp.zeros((tq, nb), F32)
    outs = []
    q = q_ref[0]
    for h in range(C_HEADS):
        s = jnp.where(valid, _nt(_head_qb(q, h), kk), NEG_INF)
        m = jnp.max(s, axis=1, keepdims=True)
        p = jnp.where(valid, jnp.exp(s - m), 0.0)
        p = p / jnp.maximum(jnp.sum(p, axis=1, keepdims=True), 1e-30)
        outs.append(_mm(p.astype(BF16), kvb))
        imp = imp + p
    for g in range(2):
        o_ref[0, :, 128 * g:128 * (g + 1)] = _pair_out(outs[2 * g], outs[2 * g + 1])
    cur = qpos // CMP_BLOCK
    score = jnp.where(jnp.logical_or(blk == 0, blk == cur), FORCED_SCORE, jnp.where(blk < cur, imp, -1.0))
    chosen = jnp.zeros((tq, nb), F32)
    blkf = blk.astype(F32)
    for _ in range(topn):
        mx = jnp.max(score, axis=1, keepdims=True)
        idx = jnp.min(jnp.where(score == mx, blkf, float(nb)), axis=1, keepdims=True)
        hit = blkf == idx
        chosen = jnp.where(hit, 1.0, chosen)
        score = jnp.where(hit, -3.0e38, score)
    msk_ref[0] = chosen


def _cmp_attend(cq, kvcmp, qpos0, topn, tq):
    B, L, _ = cq.shape
    nb = kvcmp.shape[1]
    return pl.pallas_call(
        functools.partial(_cmp_kernel, tq, qpos0, topn),
        grid=(B, L // tq),
        in_specs=[pl.BlockSpec((1, tq, 256), lambda b, t: (b, t, 0)),
                  pl.BlockSpec((1, nb, 128), lambda b, t: (b, 0, 0))],
        out_specs=[pl.BlockSpec((1, tq, 256), lambda b, t: (b, t, 0)),
                   pl.BlockSpec((1, tq, nb), lambda b, t: (b, t, 0))],
        out_shape=[jax.ShapeDtypeStruct((B, L, 256), F32), jax.ShapeDtypeStruct((B, L, nb), F32)],
        compiler_params=_cp(("arbitrary", "arbitrary")),
        name="cmp_attend_select",
    )(cq, kvcmp)


def _cmp_weight(w_cmp):
    z = jnp.zeros((CMP_BLOCK, C_DH, C_DH), w_cmp.dtype)
    k_part = jnp.stack([jnp.concatenate([w_cmp[0], z], axis=-1), jnp.concatenate([z, w_cmp[1]], axis=-1)], axis=1)
    return k_part.reshape(CMP_BLOCK * 2 * C_DH, 2 * C_DH).astype(BF16)


def _summ_kernel(x_ref, w_ref, o_ref):
    o_ref[...] = _mm(x_ref[...].astype(BF16), w_ref[...])


def _summaries_prompt(cmp_rows, wc):
    L = cmp_rows.shape[0]
    nb = L // CMP_BLOCK
    kdim = CMP_BLOCK * 128
    tr = min(nb, 128)
    return pl.pallas_call(
        _summ_kernel,
        grid=(nb // tr,),
        in_specs=[pl.BlockSpec((tr, kdim), lambda i: (i, 0)), pl.BlockSpec((kdim, 128), lambda i: (0, 0))],
        out_specs=pl.BlockSpec((tr, 128), lambda i: (i, 0)),
        out_shape=jax.ShapeDtypeStruct((nb, 128), F32),
        compiler_params=_cp(("arbitrary",)),
        name="cmp_summaries_prompt",
    )(cmp_rows.reshape(nb, kdim), wc)


def _summ_paged_kernel(G, pt_ref, w_ref, *rest):
    pages = rest[:G]
    o_ref = rest[G]
    stage = rest[G + 1]
    for g in range(G):
        stage[g:g + 1, :] = pages[g][...]
    half = stage.shape[1] // 2
    o_ref[0, :, 0:128] = _mm(stage[:, 0:half].astype(BF16), w_ref[...])
    o_ref[0, :, 128:256] = _mm(stage[:, half:].astype(BF16), w_ref[...])


def _summaries_paged(pool, layer, page_table, wc, G):
    depth, n_pool, page = pool.shape[:3]
    B, npg = page_table.shape
    width = page * 128
    view = pool.reshape(depth, n_pool, 1, width)
    pt = page_table.reshape(-1)

    def page_spec(g):
        return pl.BlockSpec((None, None, 1, width),
                            lambda b, s, pt: (layer, pt[b * npg + s * G + g], 0, 0))

    grid_spec = pltpu.PrefetchScalarGridSpec(
        num_scalar_prefetch=1, grid=(B, npg // G),
        in_specs=[pl.BlockSpec(wc.shape, lambda b, s, pt: (0, 0))] + [page_spec(g) for g in range(G)],
        out_specs=pl.BlockSpec((1, G, 256), lambda b, s, pt: (b, s, 0)),
        scratch_shapes=[pltpu.VMEM((G, width), F32)])
    out = pl.pallas_call(
        functools.partial(_summ_paged_kernel, G),
        grid_spec=grid_spec,
        out_shape=jax.ShapeDtypeStruct((B, npg, 256), F32),
        compiler_params=_cp(("arbitrary", "arbitrary")),
        name="cmp_summaries_paged",
    )(pt, wc, *([view] * G))
    return out.reshape(B, npg * (page // CMP_BLOCK), 128)


def _hgrn_kernel(C, nchunk, q_ref, k_ref, v_ref, g_ref, s0_ref, o_ref, st_ref, st_sc):
    t = pl.program_id(1)

    @pl.when(t == 0)
    def _():
        st_sc[...] = s0_ref[0]

    W = B_HEADS * B_DK
    head_mask = (_row((W, W)) // B_DV) == (_lane((W, W)) // B_DK)
    ones_blk = jnp.where(head_mask, 1.0, 0.0).astype(BF16)
    tri = jnp.where(_row((C, C)) >= _lane((C, C)), 1.0, 0.0)
    srow = _row((C, W))

    def chunk(c, carry):
        r0 = pl.multiple_of(c * C, C)
        q = q_ref[0, pl.ds(r0, C), :]
        k = k_ref[0, pl.ds(r0, C), :]
        v = v_ref[0, pl.ds(r0, C), :]
        g = g_ref[0, pl.ds(r0, C), :]
        G = lax.dot_general(tri, g, (((1,), (0,)), ((), ())), preferred_element_type=F32,
                            precision=lax.Precision.HIGHEST)
        st = st_sc[...]
        ps = []
        for tt in range(C):
            d = jnp.where(srow <= tt, G[tt:tt + 1, :] - G, NEG_INF)
            ps.append(q[tt:tt + 1, :] * jnp.exp(d) * k)
        P = jnp.concatenate(ps, axis=0)
        R = _mm(P.astype(BF16), ones_blk)
        o_intra = jnp.sum(R.reshape(C, C, W) * v[None, :, :], axis=1)
        o_inter = _nt((q * jnp.exp(G)).astype(BF16), st.astype(BF16))
        o_ref[0, pl.ds(r0, C), :] = o_intra + o_inter
        g_last = G[C - 1:C, :]
        upd = _tn(v.astype(BF16), (k * jnp.exp(g_last - G)).astype(BF16))
        st_sc[...] = st * jnp.exp(g_last) + jnp.where(head_mask, upd, 0.0)
        return carry

    lax.fori_loop(0, nchunk, chunk, 0)

    @pl.when(t == pl.num_programs(1) - 1)
    def _():
        st_ref[0] = st_sc[...]


def _hgrn(hq, hk, hv, hf, s0t, C, tr):
    B, L, W = hq.shape
    seq = pl.BlockSpec((1, tr, W), lambda b, t: (b, t, 0))
    st = pl.BlockSpec((1, W, W), lambda b, t: (b, 0, 0))
    return pl.pallas_call(
        functools.partial(_hgrn_kernel, C, tr // C),
        grid=(B, L // tr),
        in_specs=[seq, seq, seq, seq, st],
        out_specs=[seq, st],
        out_shape=[jax.ShapeDtypeStruct((B, L, W), F32), jax.ShapeDtypeStruct((B, W, W), F32)],
        scratch_shapes=[pltpu.VMEM((W, W), F32)],
        compiler_params=_cp(("arbitrary", "arbitrary")),
        name="hgrn2",
    )(hq, hk, hv, hf, s0t)


def _state_to_blockdiag_t(s):
    B = s.shape[0]
    eye = jnp.eye(B_HEADS, dtype=s.dtype)
    return jnp.einsum("bhkv,hg->bhvgk", s, eye).reshape(B, B_HEADS * B_DV, B_HEADS * B_DK)


def _blockdiag_t_to_state(st):
    B = st.shape[0]
    st5 = st.reshape(B, B_HEADS, B_DV, B_HEADS, B_DK)
    return jnp.stack([st5[:, h, :, h, :] for h in range(B_HEADS)], axis=1).transpose(0, 1, 3, 2)


def _diff_dec_kernel(G, page, lam_init, pt_ref, q_ref, kn_ref, vn_ref, dl_ref, *rest):
    kps, vps = rest[:G], rest[G:2 * G]
    o_ref = rest[2 * G]
    m_sc, l_sc, acc_sc = rest[2 * G + 1:]
    s_idx = pl.program_id(1)
    nq = q_ref.shape[1]

    @pl.when(s_idx == 0)
    def _():
        m_sc[...] = jnp.full(m_sc.shape, NEG_INF, F32)
        l_sc[...] = jnp.zeros(l_sc.shape, F32)
        acc_sc[...] = jnp.zeros(acc_sc.shape, F32)

    qparts = []
    for h in range(A_HEADS):
        qp = q_ref[0, :, 128 * h:128 * (h + 1)]
        lane = _lane(qp.shape)
        qparts += [jnp.where(lane < 64, qp, 0.0), jnp.where(lane >= 64, qp, 0.0)]
    qall = jnp.concatenate(qparts, axis=0).astype(BF16)
    rows = qall.shape[0]
    rpp = page * A_HEADS
    same_head = (_lane((rows, G * rpp)) % A_HEADS) == (_row((rows, G * rpp)) // (2 * nq))
    s = jnp.concatenate([_nt(qall, kps[g][...].astype(BF16)) for g in range(G)], axis=1)
    s = jnp.where(same_head, s, NEG_INF)
    p, alpha, m_next, l_next = _online(s, m_sc[:, :1], l_sc[:, :1])
    pb = p.astype(BF16)
    pv = _mm(pb[:, 0:rpp], vps[0][...].astype(BF16))
    for g in range(1, G):
        pv = pv + _mm(pb[:, rpp * g:rpp * (g + 1)], vps[g][...].astype(BF16))
    acc_sc[...] = acc_sc[...] * alpha + pv
    m_sc[...] = jnp.broadcast_to(m_next, m_sc.shape)
    l_sc[...] = jnp.broadcast_to(l_next, l_sc.shape)

    @pl.when(s_idx == pl.num_programs(1) - 1)
    def _():
        lam = _diff_lambda(dl_ref[...], lam_init)
        kn = jnp.concatenate([kn_ref[0, :, 128 * h:128 * (h + 1)] for h in range(A_HEADS)], axis=0)
        vn = jnp.concatenate([vn_ref[0, :, 128 * h:128 * (h + 1)] for h in range(A_HEADS)], axis=0)
        nk = A_HEADS * nq
        r, c = _row((rows, nk)), _lane((rows, nk))
        ok = jnp.logical_and(c // nq == r // (2 * nq), c % nq <= r % nq)
        sn = jnp.where(ok, _nt(qall, kn.astype(BF16)), NEG_INF)
        p2, alpha2, m2, l2 = _online(sn, m_sc[:, :1], l_sc[:, :1])
        o = (acc_sc[...] * alpha2 + _mm(p2.astype(BF16), vn.astype(BF16))) / l2
        for h in range(A_HEADS):
            o_ref[0, :, 128 * h:128 * (h + 1)] = (o[2 * nq * h:2 * nq * h + nq]
                                                  - lam * o[2 * nq * h + nq:2 * nq * (h + 1)])


def _diff_decode(aq, ak_new, av_new, dl, lam_init, pool_k, pool_v, layer, page_table, G):
    B, nq, _ = aq.shape
    depth, n_pool, page = pool_k.shape[:3]
    npg = page_table.shape[1]
    rpp = page * A_HEADS
    kview = pool_k.reshape(depth, n_pool, rpp, 128)
    vview = pool_v.reshape(depth, n_pool, rpp, 128)
    pt = page_table.reshape(-1)
    rows = A_HEADS * 2 * nq

    def page_spec(g):
        return pl.BlockSpec((None, None, rpp, 128), lambda b, s, pt: (layer, pt[b * npg + s * G + g], 0, 0))

    tok = pl.BlockSpec((1, nq, 512), lambda b, s, pt: (b, 0, 0))
    grid_spec = pltpu.PrefetchScalarGridSpec(
        num_scalar_prefetch=1, grid=(B, npg // G),
        in_specs=[tok, tok, tok, pl.BlockSpec(dl.shape, lambda b, s, pt: (0, 0))]
        + [page_spec(g) for g in range(G)] * 2,
        out_specs=tok,
        scratch_shapes=[pltpu.VMEM((rows, LANES), F32), pltpu.VMEM((rows, LANES), F32),
                        pltpu.VMEM((rows, 128), F32)])
    return pl.pallas_call(
        functools.partial(_diff_dec_kernel, G, page, lam_init),
        grid_spec=grid_spec,
        out_shape=jax.ShapeDtypeStruct((B, nq, 512), F32),
        compiler_params=_cp(("arbitrary", "arbitrary")),
        name="diff_decode",
    )(pt, aq, ak_new, av_new, dl, *([kview] * G), *([vview] * G))


def _sel_dec_kernel(G, page, pt_ref, q_ref, new_ref, msk_ref, *rest):
    pages = rest[:G]
    o_ref = rest[G]
    m_sc, l_sc, acc_sc = rest[G + 1:]
    s_idx = pl.program_id(1)
    nq = q_ref.shape[1]
    rows = C_HEADS * nq
    bpp = page // CMP_BLOCK

    @pl.when(s_idx == 0)
    def _():
        m_sc[...] = jnp.full(m_sc.shape, NEG_INF, F32)
        l_sc[...] = jnp.zeros(l_sc.shape, F32)
        acc_sc[...] = jnp.zeros(acc_sc.shape, F32)

    qall = _stack_low_q(q_ref[0])
    nb = msk_ref.shape[2]
    nkeys = G * page
    msk = msk_ref[0]
    msk4 = jnp.concatenate([msk] * C_HEADS, axis=0).astype(BF16)
    expand = jnp.where(_row((nb, nkeys)) == s_idx * (G * bpp) + _lane((nb, nkeys)) // CMP_BLOCK, 1.0, 0.0)
    valid = _mm(msk4, expand.astype(BF16)) > 0.5
    kvs = [pages[g][...].astype(BF16) for g in range(G)]
    s = jnp.concatenate([_mm(qall, kvs[g]) for g in range(G)], axis=1)
    s = jnp.where(valid, s, NEG_INF)
    p, alpha, m_next, l_next = _online(s, m_sc[:, :1], l_sc[:, :1])
    pb = p.astype(BF16)
    pv = _nt(pb[:, 0:page], kvs[0])
    for g in range(1, G):
        pv = pv + _nt(pb[:, page * g:page * (g + 1)], kvs[g])
    acc_sc[...] = acc_sc[...] * alpha + pv
    m_sc[...] = jnp.broadcast_to(m_next, m_sc.shape)
    l_sc[...] = jnp.broadcast_to(l_next, l_sc.shape)

    @pl.when(s_idx == pl.num_programs(1) - 1)
    def _():
        kvn = new_ref[0].astype(BF16)
        causal = _lane((rows, nq)) <= (_row((rows, nq)) % nq)
        sn = jnp.where(causal, _nt(qall, kvn), NEG_INF)
        p2, alpha2, m2, l2 = _online(sn, m_sc[:, :1], l_sc[:, :1])
        o = (acc_sc[...] * alpha2 + _mm(p2.astype(BF16), kvn)) / l2
        for g in range(2):
            o_ref[0, :, 128 * g:128 * (g + 1)] = _pair_out(o[2 * g * nq:(2 * g + 1) * nq],
                                                           o[(2 * g + 1) * nq:(2 * g + 2) * nq])


def _sel_decode(cqr, sel_new, selmask, pool, layer, page_table, G):
    B, nq, _ = cqr.shape
    depth, n_pool, page = pool.shape[:3]
    npg = page_table.shape[1]
    nb = selmask.shape[2]
    view = _feature_major(pool)
    pt = page_table.reshape(-1)
    rows = C_HEADS * nq

    def page_spec(g):
        return pl.BlockSpec((None, None, 128, page), lambda b, s, pt: (layer, pt[b * npg + s * G + g], 0, 0))

    grid_spec = pltpu.PrefetchScalarGridSpec(
        num_scalar_prefetch=1, grid=(B, npg // G),
        in_specs=[pl.BlockSpec((1, nq, 256), lambda b, s, pt: (b, 0, 0)),
                  pl.BlockSpec((1, nq, 128), lambda b, s, pt: (b, 0, 0)),
                  pl.BlockSpec((1, nq, nb), lambda b, s, pt: (b, 0, 0))] + [page_spec(g) for g in range(G)],
        out_specs=pl.BlockSpec((1, nq, 256), lambda b, s, pt: (b, 0, 0)),
        scratch_shapes=[pltpu.VMEM((rows, LANES), F32), pltpu.VMEM((rows, LANES), F32),
                        pltpu.VMEM((rows, 128), F32)])
    return pl.pallas_call(
        functools.partial(_sel_dec_kernel, G, page),
        grid_spec=grid_spec,
        out_shape=jax.ShapeDtypeStruct((B, nq, 256), F32),
        compiler_params=_cp(("arbitrary", "arbitrary")),
        name="sel_decode",
    )(pt, cqr, sel_new, selmask, *([view] * G))


def _win_dec_kernel(q_ref, buf_ref, new_ref, o_ref):
    nq = q_ref.shape[1]
    W = buf_ref.shape[2]
    rows = C_HEADS * nq
    qall = _stack_low_q(q_ref[0])
    kvb = buf_ref[0].astype(BF16)
    kvn = new_ref[0].astype(BF16)
    tok = _row((rows, W)) % nq
    sb = jnp.where(_lane((rows, W)) > tok, _mm(qall, kvb), NEG_INF)
    sn = jnp.where(_lane((rows, nq)) <= _row((rows, nq)) % nq, _nt(qall, kvn), NEG_INF)
    m = jnp.maximum(jnp.max(sb, axis=1, keepdims=True), jnp.max(sn, axis=1, keepdims=True))
    pb, pn = jnp.exp(sb - m), jnp.exp(sn - m)
    l = jnp.sum(pb, axis=1, keepdims=True) + jnp.sum(pn, axis=1, keepdims=True)
    o = (_nt(pb.astype(BF16), kvb) + _mm(pn.astype(BF16), kvn)) / l
    for g in range(2):
        o_ref[0, :, 128 * g:128 * (g + 1)] = _pair_out(o[2 * g * nq:(2 * g + 1) * nq],
                                                       o[(2 * g + 1) * nq:(2 * g + 2) * nq])


def _win_decode(cqr, win_buf, win_new):
    B, nq, _ = cqr.shape
    W = win_buf.shape[2]
    return pl.pallas_call(
        _win_dec_kernel,
        grid=(B,),
        in_specs=[pl.BlockSpec((1, nq, 256), lambda b: (b, 0, 0)),
                  pl.BlockSpec((1, 128, W), lambda b: (b, 0, 0)),
                  pl.BlockSpec((1, nq, 128), lambda b: (b, 0, 0))],
        out_specs=pl.BlockSpec((1, nq, 256), lambda b: (b, 0, 0)),
        out_shape=jax.ShapeDtypeStruct((B, nq, 256), F32),
        compiler_params=_cp(("arbitrary",)),
        name="win_decode",
    )(cqr, win_buf, win_new)


def _merge_kernel(lam_init, x_ref, g1_ref, ao_ref, bo_ref, hg_ref, oc_ref, os_ref, ow_ref, cg_ref,
                  sub_ref, hn_ref, w_ref, o_ref):
    tm = x_ref.shape[0]
    sub = sub_ref[...]
    y = jnp.zeros((tm, w_ref.shape[1]), F32)
    for h in range(A_HEADS):
        a = ao_ref[:, 128 * h:128 * (h + 1)]
        a = a * lax.rsqrt(jnp.mean(a * a, axis=-1, keepdims=True) + RMS_EPS) * sub * (1.0 - lam_init)
        y = y + _mm(a.astype(BF16), w_ref[128 * h:128 * (h + 1), :])
    lane = _lane((tm, 128))
    lo = lane < 64
    hn = hn_ref[...]
    for g in range(2):
        b = bo_ref[:, 128 * g:128 * (g + 1)]
        b2 = b * b
        s_lo = jnp.sum(jnp.where(lo, b2, 0.0), axis=-1, keepdims=True)
        s_hi = jnp.sum(jnp.where(lo, 0.0, b2), axis=-1, keepdims=True)
        ms = jnp.where(lo, s_lo, s_hi) * (1.0 / B_DV)
        b = b * lax.rsqrt(ms + RMS_EPS) * hn * _silu(hg_ref[:, 128 * g:128 * (g + 1)])
        y = y + _mm(b.astype(BF16), w_ref[512 + 128 * g:512 + 128 * (g + 1), :])
    sig = jax.nn.sigmoid(cg_ref[...])
    for g in range(2):
        c = jnp.zeros((tm, 128), F32)
        for br, ref in enumerate((oc_ref, os_ref, ow_ref)):
            ge = jnp.broadcast_to(sig[:, 6 * g + br:6 * g + br + 1], (tm, 128))
            go = jnp.broadcast_to(sig[:, 6 * g + 3 + br:6 * g + 3 + br + 1], (tm, 128))
            c = c + jnp.where(lo, ge, go) * ref[:, 128 * g:128 * (g + 1)]
        y = y + _mm(c.astype(BF16), w_ref[768 + 128 * g:768 + 128 * (g + 1), :])
    o_ref[...] = x_ref[...] + g1_ref[...] * y


def _merge(lam_init, x, g1, a_o, b_o, hg, o_cmp, o_sel, o_win, cg, sub, hn2, w_bf, tm):
    rows, d = x.shape
    mod_spec = (pl.BlockSpec((1, d), lambda i: (0, 0)) if g1.shape[0] == 1
                else pl.BlockSpec((tm, d), lambda i: (i, 0)))

    def rs(w):
        return pl.BlockSpec((tm, w), lambda i: (i, 0))

    def full(a):
        return pl.BlockSpec(a.shape, lambda i: (0, 0))

    return pl.pallas_call(
        functools.partial(_merge_kernel, lam_init),
        grid=(rows // tm,),
        in_specs=[rs(d), mod_spec, rs(512), rs(256), rs(256), rs(256), rs(256), rs(256), rs(12),
                  full(sub), full(hn2), full(w_bf)],
        out_specs=rs(d),
        out_shape=jax.ShapeDtypeStruct((rows, d), F32),
        compiler_params=_cp(("arbitrary",)),
        name="merge_out_proj",
    )(x, g1, a_o, b_o, hg, o_cmp, o_sel, o_win, cg, sub, hn2, w_bf)


def _ffn_kernel(fc, x_ref, sc_ref, sh_ref, g2_ref, nw_ref, wgu_ref, wd_ref, o_ref):
    x = x_ref[...]
    hb = _rms_mod(x, nw_ref[...], sc_ref[...], sh_ref[...]).astype(BF16)
    dff = wd_ref.shape[0]
    y = jnp.zeros(x.shape, F32)
    for c in range(dff // fc):
        g = _mm(hb, wgu_ref[:, c * fc:(c + 1) * fc])
        u = _mm(hb, wgu_ref[:, dff + c * fc:dff + (c + 1) * fc])
        y = y + _mm((_silu(g) * u).astype(BF16), wd_ref[c * fc:(c + 1) * fc, :])
    o_ref[...] = x + g2_ref[...] * y


def _ffn_dense(x, sc, sh, g2, nw, wgu_bf, wd_bf, tm, fc):
    rows, d = x.shape
    mod_spec = (pl.BlockSpec((1, d), lambda i: (0, 0)) if sc.shape[0] == 1
                else pl.BlockSpec((tm, d), lambda i: (i, 0)))
    return pl.pallas_call(
        functools.partial(_ffn_kernel, fc),
        grid=(rows // tm,),
        in_specs=[pl.BlockSpec((tm, d), lambda i: (i, 0)), mod_spec, mod_spec, mod_spec,
                  pl.BlockSpec((1, d), lambda i: (0, 0)),
                  pl.BlockSpec(wgu_bf.shape, lambda i: (0, 0)),
                  pl.BlockSpec(wd_bf.shape, lambda i: (0, 0))],
        out_specs=pl.BlockSpec((tm, d), lambda i: (i, 0)),
        out_shape=jax.ShapeDtypeStruct((rows, d), F32),
        compiler_params=_cp(("arbitrary",), 56),
        name="ffn_dense",
    )(x, sc, sh, g2, nw, wgu_bf, wd_bf)


def _router_kernel(x_ref, sc_ref, sh_ref, nw_ref, wr_ref, br_ref, gate_ref, pos_ref, cnt_ref):
    h = _rms_mod(x_ref[...], nw_ref[...], sc_ref[...], sh_ref[...])
    tm = h.shape[0]
    logits = lax.dot_general(wr_ref[...], h, (((1,), (1,)), ((), ())), preferred_element_type=F32,
                             precision=lax.Precision.HIGHEST) + br_ref[...]
    ne = logits.shape[0]
    row = _row((ne, tm)).astype(F32)
    m1 = jnp.max(logits, axis=0, keepdims=True)
    i1 = jnp.min(jnp.where(logits == m1, row, float(ne)), axis=0, keepdims=True)
    hit1 = row == i1
    l2 = jnp.where(hit1, -3.0e38, logits)
    m2 = jnp.max(l2, axis=0, keepdims=True)
    i2 = jnp.min(jnp.where(l2 == m2, row, float(ne)), axis=0, keepdims=True)
    hit2 = row == i2
    e2 = jnp.exp(m2 - m1)
    den = 1.0 + e2
    gate_ref[...] = jnp.where(hit1, 1.0 / den, jnp.where(hit2, e2 / den, 0.0))
    member = jnp.logical_or(hit1, hit2)
    mem = jnp.where(member, 1.0, 0.0)
    before = jnp.where(_row((tm, tm)) < _lane((tm, tm)), 1.0, 0.0).astype(BF16)
    pos = _mm(mem.astype(BF16), before)
    pos_ref[...] = jnp.where(member, pos, -1.0)
    cnt_ref[0] = jnp.broadcast_to(jnp.sum(mem, axis=1, keepdims=True), (ne, LANES)).astype(jnp.int32)


def _router(x, sc, sh, nw, w_router, b_router, tm):
    rows, d = x.shape
    ne = w_router.shape[1]
    nt = rows // tm
    mod_spec = (pl.BlockSpec((1, d), lambda i: (0, 0)) if sc.shape[0] == 1
                else pl.BlockSpec((tm, d), lambda i: (i, 0)))
    return pl.pallas_call(
        _router_kernel,
        grid=(nt,),
        in_specs=[pl.BlockSpec((tm, d), lambda i: (i, 0)), mod_spec, mod_spec,
                  pl.BlockSpec((1, d), lambda i: (0, 0)),
                  pl.BlockSpec((ne, d), lambda i: (0, 0)),
                  pl.BlockSpec((ne, 1), lambda i: (0, 0))],
        out_specs=[pl.BlockSpec((ne, tm), lambda i: (0, i)), pl.BlockSpec((ne, tm), lambda i: (0, i)),
                   pl.BlockSpec((1, ne, LANES), lambda i: (i, 0, 0))],
        out_shape=[jax.ShapeDtypeStruct((ne, rows), F32), jax.ShapeDtypeStruct((ne, rows), F32),
                   jax.ShapeDtypeStruct((nt, ne, LANES), jnp.int32)],
        compiler_params=_cp(("arbitrary",)),
        name="moe_router",
    )(x, sc, sh, nw, w_router.T, b_router.reshape(ne, 1))


def _moe_kernel(nf, final_norm, cnt_ref, x_ref, sc_ref, sh_ref, g2_ref, nw_ref, nfin_ref, gate_ref, pos_ref,
                wg_ref, wu_ref, wd_ref, o_ref, hb_sc, acc_sc, xs_sc, ys_sc):
    i, e, f = pl.program_id(0), pl.program_id(1), pl.program_id(2)
    ne = pl.num_programs(1)
    tm = x_ref.shape[0]
    nsub = (cnt_ref[i * ne + e] + (MOE_SUB - 1)) // MOE_SUB

    @pl.when(jnp.logical_and(e == 0, f == 0))
    def _():
        hb_sc[...] = _rms_mod(x_ref[...], nw_ref[...], sc_ref[...], sh_ref[...]).astype(BF16)
        acc_sc[...] = jnp.zeros(acc_sc.shape, F32)

    def onehot(j):
        pos_row = pos_ref[pl.ds(e, 1), :]
        slot = (_row((MOE_SUB, tm)) + j * MOE_SUB).astype(F32)
        return pos_row == slot

    def body(j, carry):
        r0 = pl.multiple_of(j * MOE_SUB, MOE_SUB)

        @pl.when(f == 0)
        def _():
            sel = jnp.where(onehot(j), 1.0, 0.0).astype(BF16)
            xs_sc[pl.ds(r0, MOE_SUB), :] = _mm(sel, hb_sc[...]).astype(BF16)
            ys_sc[pl.ds(r0, MOE_SUB), :] = jnp.zeros((MOE_SUB, ys_sc.shape[1]), F32)

        xs = xs_sc[pl.ds(r0, MOE_SUB), :]
        act = (_silu(_mm(xs, wg_ref[0])) * _mm(xs, wu_ref[0])).astype(BF16)
        ys_sc[pl.ds(r0, MOE_SUB), :] += _mm(act, wd_ref[0])

        @pl.when(f == nf - 1)
        def _():
            oh = onehot(j)
            gate_row = gate_ref[pl.ds(e, 1), :]
            gr = jnp.sum(jnp.where(oh, gate_row, 0.0), axis=1, keepdims=True)
            y = ys_sc[pl.ds(r0, MOE_SUB), :] * gr
            y_hi = y.astype(BF16)
            y_lo = (y - y_hi.astype(F32)).astype(BF16)
            sel = jnp.where(oh, 1.0, 0.0).astype(BF16)
            acc_sc[...] += _tn(sel, y_hi) + _tn(sel, y_lo)
        return carry

    lax.fori_loop(0, nsub, body, 0)

    @pl.when(jnp.logical_and(e == ne - 1, f == nf - 1))
    def _():
        xo = x_ref[...] + g2_ref[...] * acc_sc[...]
        if final_norm:
            xo = xo * lax.rsqrt(jnp.mean(xo * xo, axis=-1, keepdims=True) + RMS_EPS) * nfin_ref[...]
        o_ref[...] = xo


def _moe(x, sc, sh, g2, nw, nfin, gate_t, pos_t, counts, wgu_bf, wd_bf, tm, nf, final_norm):
    rows, d = x.shape
    ne, _, dff2 = wgu_bf.shape
    dff = dff2 // 2
    fc = dff // nf
    nt = rows // tm
    mod_spec = (pl.BlockSpec((1, d), lambda i, e, f, c: (0, 0)) if sc.shape[0] == 1
                else pl.BlockSpec((tm, d), lambda i, e, f, c: (i, 0)))
    vec = pl.BlockSpec((1, d), lambda i, e, f, c: (0, 0))
    route = pl.BlockSpec((ne, tm), lambda i, e, f, c: (0, i))
    grid_spec = pltpu.PrefetchScalarGridSpec(
        num_scalar_prefetch=1, grid=(nt, ne, nf),
        in_specs=[pl.BlockSpec((tm, d), lambda i, e, f, c: (i, 0)), mod_spec, mod_spec, mod_spec, vec, vec,
                  route, route,
                  pl.BlockSpec((1, d, fc), lambda i, e, f, c: (e, 0, f)),
                  pl.BlockSpec((1, d, fc), lambda i, e, f, c: (e, 0, nf + f)),
                  pl.BlockSpec((1, fc, d), lambda i, e, f, c: (e, f, 0))],
        out_specs=pl.BlockSpec((tm, d), lambda i, e, f, c: (i, 0)),
        scratch_shapes=[pltpu.VMEM((tm, d), BF16), pltpu.VMEM((tm, d), F32),
                        pltpu.VMEM((tm, d), BF16), pltpu.VMEM((tm, d), F32)])
    return pl.pallas_call(
        functools.partial(_moe_kernel, nf, final_norm),
        grid_spec=grid_spec,
        out_shape=jax.ShapeDtypeStruct((rows, d), F32),
        compiler_params=_cp(("arbitrary", "arbitrary", "arbitrary"), 56),
        name="moe_experts",
    )(counts, x, sc, sh, g2, nw, nfin, gate_t, pos_t, wgu_bf, wgu_bf, wd_bf)


def _norm_kernel(x_ref, w_ref, o_ref):
    x = x_ref[...]
    o_ref[...] = x * lax.rsqrt(jnp.mean(x * x, axis=-1, keepdims=True) + RMS_EPS) * w_ref[...]


def _final_norm(x, w, tm):
    rows, d = x.shape
    return pl.pallas_call(
        _norm_kernel, grid=(rows // tm,),
        in_specs=[pl.BlockSpec((tm, d), lambda i: (i, 0)), pl.BlockSpec((1, d), lambda i: (0, 0))],
        out_specs=pl.BlockSpec((tm, d), lambda i: (i, 0)),
        out_shape=jax.ShapeDtypeStruct((rows, d), F32),
        compiler_params=_cp(("arbitrary",)),
        name="final_norm",
    )(x, w)


def _rope_tables(pos):
    inv = ROPE_THETA ** (-jnp.arange(0, 64, 2, dtype=F32) / 64)
    ang = pos.astype(F32)[:, None] * inv[None, :]
    cos, sin = jnp.cos(ang), jnp.sin(ang)
    return jnp.tile(cos, (1, 4)), jnp.tile(jnp.concatenate([-sin, sin], axis=1), (1, 2))


def _pick(n, pref):
    t = min(n, pref)
    while n % t:
        t //= 2
    return t


def kernel(x_prompt, x_sample, cache_diff_k, cache_diff_v, cache_nsa_cmp, cache_nsa_sel, state_nsa_win, state_hgrn, page_table, c_prompt, c_sample, w_in, w_out, w_cmp, diff_lambda, diff_subln, hgrn_lb_logits, hgrn_norm, norm_mix, norm_ffn, norm_final, w_ada, b_ada, ffn_w_gate_up, ffn_w_down, moe_w_router, moe_b_router, moe_w_gate_up, moe_w_down):
    depth = w_in.shape[0]
    _, L, D = x_prompt.shape
    Bs, Ls, _ = x_sample.shape
    page = cache_diff_k.shape[2]
    past = page_table.shape[1] * page
    rows_s = Bs * Ls

    n_c = 1 + Bs
    n_cp = -(-n_c // 8) * 8
    c_all = jnp.concatenate([c_prompt, c_sample, jnp.zeros((n_cp - n_c, D), F32)], axis=0)
    mod = _ada(c_all, w_ada, b_ada)

    cos_p, sin_p = _rope_tables(jnp.arange(L))
    cos_s, sin_s = _rope_tables(past + (jnp.arange(rows_s) % Ls))

    xp = x_prompt.reshape(L, D)
    xs = x_sample.reshape(rows_s, D)
    tmp = _pick(L, 512)
    tq_a, tk_a = _pick(L, 512), _pick(L, 512)
    tq_s, tk_s = _pick(L, 256), _pick(L, 512)
    G_pages = _pick(page_table.shape[1], 8)
    outs_p = [[] for _ in range(6)]
    outs_s = [[] for _ in range(6)]

    for l in range(depth):
        lam_init = 0.8 - 0.6 * math.exp(-0.3 * l)
        w_in_bf = w_in[l].astype(BF16)
        w_out_bf = w_out[l].astype(BF16)
        wc = _cmp_weight(w_cmp[l])
        dl = diff_lambda[l]
        sub = diff_subln[l].reshape(1, A_DV)
        hn2 = jnp.tile(hgrn_norm[l].reshape(1, B_DV), (1, 2))
        nmix = norm_mix[l].reshape(1, D)
        nffn = norm_ffn[l].reshape(1, D)
        mp = [m for m in jnp.split(mod[l, 0:1], 6, axis=-1)]
        ms = [jnp.repeat(m, Ls, axis=0) for m in jnp.split(mod[l, 1:1 + Bs], 6, axis=-1)]

        (aq, ak, av, hq, hk, hf, hv, hg, cq, cqr, cmpr, sel, win, cg) = _proj(
            l, xp, mp[1], mp[0], nmix, w_in_bf, cos_p, sin_p, hgrn_lb_logits, tmp)
        a_o = _diff_flash(aq, ak, av, dl, lam_init, tq_a, tk_a)
        s0 = jnp.zeros((1, B_HEADS * B_DV, B_HEADS * B_DK), F32)
        b_o, st = _hgrn(hq[None], hk[None], hv[None], hf[None], s0, 16, _pick(L, 512))
        kvc = _summaries_prompt(cmpr, wc)
        o_cmp, selmask = _cmp_attend(cq[None], kvc[None], 0, TOP_N, _pick(L, 256))
        o_sel = _sel_flash(cqr, sel, selmask[0], tq_s, tk_s)
        o_win = _win_prompt(cqr, win, _pick(L, WINDOW))
        xp = _merge(lam_init, xp, mp[2], a_o, b_o[0], hg, o_cmp[0], o_sel, o_win, cg, sub, hn2, w_out_bf, tmp)
        W = min(WINDOW, L)
        for i, a in enumerate((ak.reshape(1, L, A_HEADS, 2 * A_DK), av.reshape(1, L, A_HEADS, A_DV),
                               cmpr.reshape(1, L, 2, C_DH), sel.reshape(1, L, 2, C_DH),
                               win[L - W:].reshape(1, W, 2, C_DH), _blockdiag_t_to_state(st))):
            outs_p[i].append(a)

        (aq, ak, av, hq, hk, hf, hv, hg, cq, cqr, cmpr, sel, win, cg) = _proj(
            l, xs, ms[1], ms[0], nmix, w_in_bf, cos_s, sin_s, hgrn_lb_logits, rows_s)
        b3 = lambda a: a.reshape(Bs, Ls, a.shape[-1])
        a_o = _diff_decode(b3(aq).astype(F32), b3(ak), b3(av), dl, lam_init, cache_diff_k, cache_diff_v, l,
                           page_table, G_pages)
        b_o, st = _hgrn(b3(hq), b3(hk), b3(hv), b3(hf), _state_to_blockdiag_t(state_hgrn[l]), Ls, Ls)
        kvc = _summaries_paged(cache_nsa_cmp, l, page_table, wc, _pick(page_table.shape[1], 16))
        o_cmp, selmask = _cmp_attend(b3(cq).astype(F32), kvc, past, TOP_N - 1, Ls)
        o_sel = _sel_decode(b3(cqr).astype(F32), b3(sel), selmask, cache_nsa_sel, l, page_table, G_pages)
        o_win = _win_decode(b3(cqr).astype(F32), _feature_major(state_nsa_win[l]), b3(win))
        xs = _merge(lam_init, xs, ms[2], a_o.reshape(rows_s, 512), b_o.reshape(rows_s, 256), hg,
                    o_cmp.reshape(rows_s, 256), o_sel.reshape(rows_s, 256), o_win.reshape(rows_s, 256), cg,
                    sub, hn2, w_out_bf, rows_s)
        win_all = jnp.concatenate([state_nsa_win[l], win.reshape(Bs, Ls, 2, C_DH)], axis=1)[:, Ls:]
        for i, a in enumerate((ak.reshape(Bs, Ls, A_HEADS, 2 * A_DK), av.reshape(Bs, Ls, A_HEADS, A_DV),
                               cmpr.reshape(Bs, Ls, 2, C_DH), sel.reshape(Bs, Ls, 2, C_DH),
                               win_all, _blockdiag_t_to_state(st))):
            outs_s[i].append(a)

        j = l // 2
        last = l == depth - 1
        nfin = norm_final.reshape(1, D)
        if l % 2 == 0:
            wgu = ffn_w_gate_up[j].astype(BF16)
            wd = ffn_w_down[j].astype(BF16)
            fc = wd.shape[0] // 2
            xp = _ffn_dense(xp, mp[4], mp[3], mp[5], nffn, wgu, wd, tmp, fc)
            xs = _ffn_dense(xs, ms[4], ms[3], ms[5], nffn, wgu, wd, rows_s, fc)
            if last:
                xp = _final_norm(xp, nfin, tmp)
                xs = _final_norm(xs, nfin, rows_s)
        else:
            wgu = moe_w_gate_up[j].astype(BF16)
            wd = moe_w_down[j].astype(BF16)
            for which in range(2):
                x, m6, tm = (xp, mp, _pick(L, 1024)) if which == 0 else (xs, ms, rows_s)
                gate_t, pos_t, cnt = _router(x, m6[4], m6[3], nffn, moe_w_router[j], moe_b_router[j], tm)
                y = _moe(x, m6[4], m6[3], m6[5], nffn, nfin, gate_t, pos_t, cnt[:, :, 0].reshape(-1),
                         wgu, wd, tm, 2, last)
                if which == 0:
                    xp = y
                else:
                    xs = y
            if not last:
                pass

    y_prompt = xp.reshape(1, L, D)
    y_sample = xs.reshape(Bs, Ls, D)
    p_st = [jnp.stack(a, axis=0) for a in outs_p]
    s_st = [jnp.stack(a, axis=0) for a in outs_s]
    return (y_prompt, y_sample, *p_st, *s_st)
```

```python
import functools
import math

import numpy as np
import jax
import jax.numpy as jnp
from jax import lax
from jax.experimental import pallas as pl
from jax.experimental.pallas import tpu as pltpu

F32 = jnp.float32
BF16 = jnp.bfloat16

A_HEADS = 4
A_DK = 64
A_DV = 128
B_HEADS = 4
B_DK = 64
B_DV = 64
C_HEADS = 4
C_DH = 64
CMP_BLOCK = 64
TOP_N = 16
WINDOW = 512
ROPE_THETA = 10000.0
N_EXPERTS = 8
RMS_EPS = 1e-6
NEG_INF = -1e30
FORCED_SCORE = 1e4
F_FLOOR = 1e-20
QK_SCALE = 0.125 * math.log2(math.e)
LANES = 128
MOE_SUB = 128

_SPL = (512, 512, 512, 256, 256, 256, 256, 256, 128, 128, 128, 12)
_OFF = tuple(int(v) for v in np.cumsum((0,) + _SPL))
IN_COLS = _OFF[-1]


def _cp(sem, vmem_mb=48):
    return pltpu.CompilerParams(dimension_semantics=sem, vmem_limit_bytes=vmem_mb * 1024 * 1024)


def _silu(x):
    return x * jax.nn.sigmoid(x)


def _nt(a, b):
    return lax.dot_general(a, b, (((1,), (1,)), ((), ())), preferred_element_type=F32)


def _tn(a, b):
    return lax.dot_general(a, b, (((0,), (0,)), ((), ())), preferred_element_type=F32)


def _mm(a, b):
    return jnp.dot(a, b, preferred_element_type=F32)


def _lane(shape):
    return lax.broadcasted_iota(jnp.int32, shape, 1)


def _row(shape):
    return lax.broadcasted_iota(jnp.int32, shape, 0)


def _rms_mod(x, nw, sc, sh):
    h = x * lax.rsqrt(jnp.mean(x * x, axis=-1, keepdims=True) + RMS_EPS) * nw
    return h * (1.0 + sc) + sh


def _ada_kernel(c_ref, w_ref, b_ref, o_ref):
    s = _silu(c_ref[...]).astype(BF16)
    o_ref[0] = _mm(s, w_ref[0].astype(BF16)) + b_ref[0]


def _ada(c_all, w_ada, b_ada):
    depth, d, n = w_ada.shape
    r = c_all.shape[0]
    tn = 1536
    return pl.pallas_call(
        _ada_kernel,
        grid=(depth, n // tn),
        in_specs=[pl.BlockSpec((r, d), lambda l, j: (0, 0)),
                  pl.BlockSpec((1, d, tn), lambda l, j: (l, 0, j)),
                  pl.BlockSpec((1, 1, tn), lambda l, j: (l, 0, j))],
        out_specs=pl.BlockSpec((1, r, tn), lambda l, j: (l, 0, j)),
        out_shape=jax.ShapeDtypeStruct((depth, r, n), F32),
        compiler_params=_cp(("arbitrary", "arbitrary")),
        name="ada_mod",
    )(c_all, w_ada, b_ada.reshape(depth, 1, n))


def _proj_kernel(layer, x_ref, sc_ref, sh_ref, nw_ref, w_ref, cos_ref, sin_ref, lbl_ref,
                 aq_ref, ak_ref, av_ref, hq_ref, hk_ref, hf_ref, hv_ref, hg_ref,
                 cq_ref, cqr_ref, cmp_ref, sel_ref, win_ref, cg_ref):
    hb = _rms_mod(x_ref[...], nw_ref[...], sc_ref[...], sh_ref[...]).astype(BF16)
    cos = cos_ref[...]
    sin = sin_ref[...]
    lane = _lane(cos.shape)
    first = (lane % 64) < 32

    def mm(i):
        return _mm(hb, w_ref[:, _OFF[i]:_OFF[i + 1]])

    def rope(y, c, s):
        rot = jnp.where(first, pltpu.roll(y, 96, 1), pltpu.roll(y, 32, 1))
        return y * c + rot * s

    y = mm(0)
    for g in range(4):
        aq_ref[:, 128 * g:128 * (g + 1)] = (rope(y[:, 128 * g:128 * (g + 1)], cos, sin) * QK_SCALE).astype(BF16)
    y = mm(1)
    for g in range(4):
        ak_ref[:, 128 * g:128 * (g + 1)] = rope(y[:, 128 * g:128 * (g + 1)], cos, sin)
    av_ref[...] = mm(2)
    hq_ref[...] = _silu(mm(3))
    lbl = lbl_ref[...]
    e = jnp.exp(lbl - jnp.max(lbl, axis=0, keepdims=True))
    sm = e / jnp.sum(e, axis=0, keepdims=True)
    lb = jnp.sum(sm[0:layer + 1], axis=0, keepdims=True) - sm[0:1]
    f = lb + (1.0 - lb) * jax.nn.sigmoid(mm(4))
    hf_ref[...] = jnp.log(jnp.maximum(f, F_FLOOR))
    hk_ref[...] = 1.0 - f
    hv_ref[...] = mm(5)
    hg_ref[...] = mm(6)
    y = mm(7)
    cq_ref[...] = (y * QK_SCALE).astype(BF16)
    for g in range(2):
        cqr_ref[:, 128 * g:128 * (g + 1)] = (rope(y[:, 128 * g:128 * (g + 1)], cos, sin) * QK_SCALE).astype(BF16)
    cmp_ref[...] = mm(8)
    cos_k = jnp.where(lane < 64, cos, 1.0)
    sin_k = jnp.where(lane < 64, sin, 0.0)
    sel_ref[...] = rope(mm(9), cos_k, sin_k)
    win_ref[...] = rope(mm(10), cos_k, sin_k)
    cg_ref[...] = mm(11)


def _proj(layer, x, sc, sh, nw, w_bf, cos, sin, lbl, tm):
    rows, d = x.shape
    mrows = sc.shape[0]
    mod_spec = (pl.BlockSpec((1, d), lambda i: (0, 0)) if mrows == 1
                else pl.BlockSpec((tm, d), lambda i: (i, 0)))
    widths = (512, 512, 512, 256, 256, 256, 256, 256, 256, 256, 128, 128, 128, 12)
    dtypes = (BF16, F32, F32, F32, F32, F32, F32, F32, BF16, BF16, F32, F32, F32, F32)
    return pl.pallas_call(
        functools.partial(_proj_kernel, layer),
        grid=(rows // tm,),
        in_specs=[pl.BlockSpec((tm, d), lambda i: (i, 0)), mod_spec, mod_spec,
                  pl.BlockSpec((1, d), lambda i: (0, 0)),
                  pl.BlockSpec((d, IN_COLS), lambda i: (0, 0)),
                  pl.BlockSpec((tm, LANES), lambda i: (i, 0)),
                  pl.BlockSpec((tm, LANES), lambda i: (i, 0)),
                  pl.BlockSpec(lbl.shape, lambda i: (0, 0))],
        out_specs=[pl.BlockSpec((tm, w), lambda i: (i, 0)) for w in widths],
        out_shape=[jax.ShapeDtypeStruct((rows, w), dt) for w, dt in zip(widths, dtypes)],
        compiler_params=_cp(("arbitrary",), 56),
        name="in_proj",
    )(x, sc, sh, nw, w_bf, cos, sin, lbl)


def _online(s, m_prev, l_prev):
    m_next = jnp.maximum(m_prev, jnp.max(s, axis=1, keepdims=True))
    p = jnp.exp2(s - m_next)
    alpha = jnp.exp2(m_prev - m_next)
    l_next = alpha * l_prev + jnp.sum(p, axis=1, keepdims=True)
    return p, alpha, m_next, l_next


def _flash_step(s, valid, m_prev, rhs_aug):
    n = s.shape[1]
    chunks = [s[:, LANES * c:LANES * (c + 1)] for c in range(n // LANES)]
    if valid is not None:
        chunks = [jnp.where(valid[:, LANES * c:LANES * (c + 1)], ch, NEG_INF) for c, ch in enumerate(chunks)]
    mx = chunks[0]
    for ch in chunks[1:]:
        mx = jnp.maximum(mx, ch)
    m_next = jnp.maximum(m_prev, jnp.max(mx, axis=1, keepdims=True))
    p = jnp.concatenate([jnp.exp2(ch - m_next).astype(BF16) for ch in chunks], axis=1)
    return _mm(p, rhs_aug), m_next, jnp.exp2(m_prev - m_next)


def _diff_lambda(dl, lam_init):
    a = jnp.sum(dl[0:1] * dl[1:2], axis=1, keepdims=True)
    b = jnp.sum(dl[2:3] * dl[3:4], axis=1, keepdims=True)
    return jnp.exp(a) - jnp.exp(b) + lam_init


def _causal_pairs(nq, tq, tk):
    ii, jj = [], []
    for i in range(nq):
        for j in range(((i + 1) * tq - 1) // tk + 1):
            ii.append(i)
            jj.append(j)
    return jnp.asarray(ii, jnp.int32), jnp.asarray(jj, jnp.int32)


def _diff_flash_kernel(tq, tk, lam_init, ii_ref, jj_ref, q_ref, k_ref, v_ref, dl_ref, o_ref,
                       m_sc, l_sc, acc_sc):
    s_idx = pl.program_id(1)
    i = ii_ref[s_idx]
    j = jj_ref[s_idx]
    hpg = q_ref.shape[1] // 128

    @pl.when(j == 0)
    def _():
        m_sc[...] = jnp.full(m_sc.shape, NEG_INF, F32)
        l_sc[...] = jnp.zeros(l_sc.shape, F32)
        acc_sc[...] = jnp.zeros(acc_sc.shape, F32)

    ones = jnp.ones((tk, LANES), BF16)
    low = _lane((tq, 128)) < 64

    def step(masked):
        valid = (j * tk + _lane((tq, tk))) <= (i * tq + _row((tq, tk))) if masked else None
        for h in range(hpg):
            q = q_ref[:, 128 * h:128 * (h + 1)]
            kb = k_ref[:, 128 * h:128 * (h + 1)].astype(BF16)
            v_aug = jnp.concatenate([v_ref[:, 128 * h:128 * (h + 1)].astype(BF16), ones], axis=1)
            for mi in range(2):
                c = 2 * h + mi
                qm = jnp.where(low if mi == 0 else jnp.logical_not(low), q, jnp.zeros_like(q))
                pv, m_next, alpha = _flash_step(_nt(qm, kb), valid, m_sc[c], v_aug)
                acc_sc[c] = acc_sc[c] * alpha + pv[:, :128]
                l_sc[c] = l_sc[c] * alpha + pv[:, 128:]
                m_sc[c] = m_next

    needs_mask = (j * tk + tk - 1) > (i * tq)
    pl.when(needs_mask)(lambda: step(True))
    pl.when(jnp.logical_not(needs_mask))(lambda: step(False))

    @pl.when(j == ((i + 1) * tq - 1) // tk)
    def _():
        lam = _diff_lambda(dl_ref[...], lam_init)
        for h in range(hpg):
            o_ref[:, 128 * h:128 * (h + 1)] = (acc_sc[2 * h] / l_sc[2 * h]
                                               - lam * (acc_sc[2 * h + 1] / l_sc[2 * h + 1]))


def _diff_flash(aq, ak, av, dl, lam_init, tq, tk, hpg):
    L = aq.shape[0]
    nq = L // tq
    ii, jj = _causal_pairs(nq, tq, tk)
    w = 128 * hpg
    grid_spec = pltpu.PrefetchScalarGridSpec(
        num_scalar_prefetch=2,
        grid=(A_HEADS // hpg, int(ii.shape[0])),
        in_specs=[pl.BlockSpec((tq, w), lambda h, s, ii, jj: (ii[s], h)),
                  pl.BlockSpec((tk, w), lambda h, s, ii, jj: (jj[s], h)),
                  pl.BlockSpec((tk, w), lambda h, s, ii, jj: (jj[s], h)),
                  pl.BlockSpec(dl.shape, lambda h, s, ii, jj: (0, 0))],
        out_specs=pl.BlockSpec((tq, w), lambda h, s, ii, jj: (ii[s], h)),
        scratch_shapes=[pltpu.VMEM((2 * hpg, tq, LANES), F32), pltpu.VMEM((2 * hpg, tq, LANES), F32),
                        pltpu.VMEM((2 * hpg, tq, 128), F32)])
    return pl.pallas_call(
        functools.partial(_diff_flash_kernel, tq, tk, lam_init),
        grid_spec=grid_spec,
        out_shape=jax.ShapeDtypeStruct((L, 512), F32),
        compiler_params=_cp(("arbitrary", "arbitrary")),
        name="diff_flash",
    )(ii, jj, aq, ak, av, dl)


def _kk(kv):
    return jnp.where(_lane(kv.shape) < 64, kv, pltpu.roll(kv, 64, 1)).astype(BF16)


def _head_q(q_ref_or_val, h):
    qp = q_ref_or_val[:, 128 * (h // 2):128 * (h // 2 + 1)]
    lane = _lane(qp.shape)
    keep = (lane < 64) if h % 2 == 0 else (lane >= 64)
    return jnp.where(keep, qp, jnp.zeros_like(qp))


def _head_qb(q_ref_or_val, h):
    return _head_q(q_ref_or_val, h).astype(BF16)


def _pair_out(o_even, o_odd):
    return jnp.where(_lane(o_even.shape) < 64, pltpu.roll(o_even, 64, 1), o_odd)


def _stack_low_q(q):
    parts = []
    for g in range(C_HEADS // 2):
        qp = q[:, 128 * g:128 * (g + 1)]
        low = _lane(qp.shape) < 64
        parts += [jnp.where(low, qp, 0.0), jnp.where(low, pltpu.roll(qp, 64, 1), 0.0)]
    return jnp.concatenate(parts, axis=0).astype(BF16)


def _feature_major(a):
    n = a.ndim
    t = jnp.transpose(a, tuple(range(n - 3)) + (n - 2, n - 1, n - 3))
    return t.reshape(a.shape[:-3] + (2 * C_DH, a.shape[-3]))


def _sel_flash_kernel(tq, tk, ii_ref, jj_ref, q_ref, kv_ref, msk_ref, o_ref, m_sc, l_sc, acc_sc):
    s_idx = pl.program_id(0)
    i = ii_ref[s_idx]
    j = jj_ref[s_idx]
    bpt = tk // CMP_BLOCK

    @pl.when(j == 0)
    def _():
        m_sc[...] = jnp.full(m_sc.shape, NEG_INF, F32)
        l_sc[...] = jnp.zeros(l_sc.shape, F32)
        acc_sc[...] = jnp.zeros(acc_sc.shape, F32)

    kv = kv_ref[...]
    kv_aug = jnp.concatenate([kv.astype(BF16), jnp.ones((tk, LANES), BF16)], axis=1)
    kk = _kk(kv)
    nbl = msk_ref.shape[1]
    boff = (j * bpt) % nbl
    expand = jnp.where(_row((nbl, tk)) == boff + _lane((nbl, tk)) // CMP_BLOCK, 1.0, 0.0).astype(BF16)
    sel = _mm(msk_ref[...].astype(BF16), expand) > 0.5

    def step(masked):
        valid = sel
        if masked:
            valid = jnp.logical_and(sel, (j * tk + _lane((tq, tk))) <= (i * tq + _row((tq, tk))))
        for h in range(C_HEADS):
            pv, m_next, alpha = _flash_step(_nt(_head_qb(q_ref, h), kk), valid, m_sc[h], kv_aug)
            acc_sc[h] = acc_sc[h] * alpha + pv[:, :128]
            l_sc[h] = l_sc[h] * alpha + pv[:, 128:]
            m_sc[h] = m_next

    needs_mask = (j * tk + tk - 1) > (i * tq)
    pl.when(needs_mask)(lambda: step(True))
    pl.when(jnp.logical_not(needs_mask))(lambda: step(False))

    @pl.when(j == ((i + 1) * tq - 1) // tk)
    def _():
        for g in range(2):
            o_ref[:, 128 * g:128 * (g + 1)] = _pair_out(acc_sc[2 * g] / l_sc[2 * g],
                                                        acc_sc[2 * g + 1] / l_sc[2 * g + 1])


def _sel_flash(cqr, sel, selmask, tq, tk):
    L = cqr.shape[0]
    nb = selmask.shape[1]
    nbl = min(nb, LANES)
    bpt = tk // CMP_BLOCK
    ii, jj = _causal_pairs(L // tq, tq, tk)
    grid_spec = pltpu.PrefetchScalarGridSpec(
        num_scalar_prefetch=2,
        grid=(int(ii.shape[0]),),
        in_specs=[pl.BlockSpec((tq, 256), lambda s, ii, jj: (ii[s], 0)),
                  pl.BlockSpec((tk, 128), lambda s, ii, jj: (jj[s], 0)),
                  pl.BlockSpec((tq, nbl), lambda s, ii, jj: (ii[s], (jj[s] * bpt) // nbl))],
        out_specs=pl.BlockSpec((tq, 256), lambda s, ii, jj: (ii[s], 0)),
        scratch_shapes=[pltpu.VMEM((4, tq, LANES), F32), pltpu.VMEM((4, tq, LANES), F32),
                        pltpu.VMEM((4, tq, 128), F32)])
    return pl.pallas_call(
        functools.partial(_sel_flash_kernel, tq, tk),
        grid_spec=grid_spec,
        out_shape=jax.ShapeDtypeStruct((L, 256), F32),
        compiler_params=_cp(("arbitrary",)),
        name="sel_flash",
    )(ii, jj, cqr, sel, selmask)


def _win_prompt_kernel(t, q_ref, kp_ref, kc_ref, o_ref):
    i = pl.program_id(0)
    kvp = kp_ref[...]
    kvc = kc_ref[...]
    kkp, kkc = _kk(kvp), _kk(kvc)
    r, c = _row((t, t)), _lane((t, t))
    vp = jnp.logical_and(c > r, i > 0)
    vc = c <= r
    outs = []
    for h in range(C_HEADS):
        qh = _head_qb(q_ref, h)
        sp =jnp.where(vp, _nt(qh, kkp), NEG_INF)
        sc = jnp.where(vc, _nt(qh, kkc), NEG_INF)
        m = jnp.maximum(jnp.max(sp, axis=1, keepdims=True), jnp.max(sc, axis=1, keepdims=True))
        pp = jnp.exp2(sp - m)
        pc = jnp.exp2(sc - m)
        l = jnp.sum(pp, axis=1, keepdims=True) + jnp.sum(pc, axis=1, keepdims=True)
        outs.append((_mm(pp.astype(BF16), kvp.astype(BF16)) + _mm(pc.astype(BF16), kvc.astype(BF16))) / l)
    for g in range(2):
        o_ref[:, 128 * g:128 * (g + 1)] = _pair_out(outs[2 * g], outs[2 * g + 1])


def _win_prompt(cqr, win, t):
    L = cqr.shape[0]
    return pl.pallas_call(
        functools.partial(_win_prompt_kernel, t),
        grid=(L // t,),
        in_specs=[pl.BlockSpec((t, 256), lambda i: (i, 0)),
                  pl.BlockSpec((t, 128), lambda i: (jnp.maximum(i - 1, 0), 0)),
                  pl.BlockSpec((t, 128), lambda i: (i, 0))],
        out_specs=pl.BlockSpec((t, 256), lambda i: (i, 0)),
        out_shape=jax.ShapeDtypeStruct((L, 256), F32),
        compiler_params=_cp(("arbitrary",)),
        name="win_prompt",
    )(cqr, win, win)


def _cmp_kernel(tq, qpos0, topn, q_ref, kv_ref, o_ref, msk_ref):
    t = pl.program_id(1)
    kv = kv_ref[0]
    nb = kv.shape[0]
    kvb = kv.astype(BF16)
    kk = _kk(kv)
    blk = _lane((tq, nb))
    qpos = qpos0 + t * tq + _row((tq, nb))
    valid = (blk * CMP_BLOCK + (CMP_BLOCK - 1)) <= qpos
    imp = jnp.zeros((tq, nb), F32)
    outs = []
    q = q_ref[0]
    for h in range(C_HEADS):
        s = jnp.where(valid, _nt(_head_qb(q, h), kk), NEG_INF)
        m = jnp.max(s, axis=1, keepdims=True)
        p = jnp.where(valid, jnp.exp2(s - m), 0.0)
        p = p / jnp.maximum(jnp.sum(p, axis=1, keepdims=True), 1e-30)
        outs.append(_mm(p.astype(BF16), kvb))
        imp = imp + p
    for g in range(2):
        o_ref[0, :, 128 * g:128 * (g + 1)] = _pair_out(outs[2 * g], outs[2 * g + 1])
    cur = qpos // CMP_BLOCK
    score = jnp.where(jnp.logical_or(blk == 0, blk == cur), FORCED_SCORE, jnp.where(blk < cur, imp, -1.0))
    chosen = jnp.zeros((tq, nb), F32)
    blkf = blk.astype(F32)
    for _ in range(topn):
        mx = jnp.max(score, axis=1, keepdims=True)
        idx = jnp.min(jnp.where(score == mx, blkf, float(nb)), axis=1, keepdims=True)
        hit = blkf == idx
        chosen = jnp.where(hit, 1.0, chosen)
        score = jnp.where(hit, -3.0e38, score)
    msk_ref[0] = chosen


def _cmp_attend(cq, kvcmp, qpos0, topn, tq):
    B, L, _ = cq.shape
    nb = kvcmp.shape[1]
    return pl.pallas_call(
        functools.partial(_cmp_kernel, tq, qpos0, topn),
        grid=(B, L // tq),
        in_specs=[pl.BlockSpec((1, tq, 256), lambda b, t: (b, t, 0)),
                  pl.BlockSpec((1, nb, 128), lambda b, t: (b, 0, 0))],
        out_specs=[pl.BlockSpec((1, tq, 256), lambda b, t: (b, t, 0)),
                   pl.BlockSpec((1, tq, nb), lambda b, t: (b, t, 0))],
        out_shape=[jax.ShapeDtypeStruct((B, L, 256), F32), jax.ShapeDtypeStruct((B, L, nb), F32)],
        compiler_params=_cp(("arbitrary", "arbitrary")),
        name="cmp_attend_select",
    )(cq, kvcmp)


def _cmp_weight(w_cmp):
    z = jnp.zeros((CMP_BLOCK, C_DH, C_DH), w_cmp.dtype)
    k_part = jnp.stack([jnp.concatenate([w_cmp[0], z], axis=-1), jnp.concatenate([z, w_cmp[1]], axis=-1)], axis=1)
    return k_part.reshape(CMP_BLOCK * 2 * C_DH, 2 * C_DH).astype(BF16)


def _summ_kernel(x_ref, w_ref, o_ref):
    o_ref[...] = _mm(x_ref[...].astype(BF16), w_ref[...])


def _summaries_prompt(cmp_rows, wc):
    L = cmp_rows.shape[0]
    nb = L // CMP_BLOCK
    kdim = CMP_BLOCK * 128
    tr = min(nb, 128)
    return pl.pallas_call(
        _summ_kernel,
        grid=(nb // tr,),
        in_specs=[pl.BlockSpec((tr, kdim), lambda i: (i, 0)), pl.BlockSpec((kdim, 128), lambda i: (0, 0))],
        out_specs=pl.BlockSpec((tr, 128), lambda i: (i, 0)),
        out_shape=jax.ShapeDtypeStruct((nb, 128), F32),
        compiler_params=_cp(("arbitrary",)),
        name="cmp_summaries_prompt",
    )(cmp_rows.reshape(nb, kdim), wc)


def _cmp_weight_by_feature(w_cmp, bpp):
    eye_b = jnp.eye(bpp, dtype=w_cmp.dtype)
    eye_c = jnp.eye(2, dtype=w_cmp.dtype)
    t = jnp.einsum("cjde,bB,cC->cdbjBCe", w_cmp, eye_b, eye_c)
    return t.reshape(2 * C_DH, bpp * CMP_BLOCK, bpp * 2 * C_DH).astype(BF16)


def _summ_paged_kernel(npg, layer, pt_ref, w_ref, pool_ref, o_ref, stage, sem):
    b = pl.program_id(0)
    slot = b % 2

    def page_copy(bb, p, sl):
        return pltpu.make_async_copy(pool_ref.at[layer, pt_ref[bb * npg + p]],
                                     stage.at[sl, :, pl.ds(p, 1), :], sem.at[sl])

    def start_all(bb, sl):
        lax.fori_loop(0, npg, lambda p, c: (page_copy(bb, p, sl).start(), c)[1], 0)

    def wait_all(bb, sl):
        lax.fori_loop(0, npg, lambda p, c: (page_copy(bb, p, sl).wait(), c)[1], 0)

    @pl.when(b == 0)
    def _():
        start_all(0, 0)

    @pl.when(b + 1 < pl.num_programs(0))
    def _():
        start_all(b + 1, 1 - slot)

    wait_all(b, slot)

    def body(cd, acc):
        return acc + _mm(stage[slot, cd].astype(BF16), w_ref[cd])

    o_ref[0] = lax.fori_loop(0, stage.shape[1], body, jnp.zeros(o_ref.shape[1:], F32), unroll=4)


def _summaries_paged(pool, layer, page_table, w_feat):
    depth, n_pool, page = pool.shape[:3]
    B, npg = page_table.shape
    nfeat = 2 * C_DH
    view = _feature_major(pool).reshape(depth, n_pool, nfeat, 1, page)
    wout = w_feat.shape[2]
    grid_spec = pltpu.PrefetchScalarGridSpec(
        num_scalar_prefetch=1, grid=(B,),
        in_specs=[pl.BlockSpec(w_feat.shape, lambda b, pt: (0, 0, 0)),
                  pl.BlockSpec(memory_space=pl.ANY)],
        out_specs=pl.BlockSpec((1, npg, wout), lambda b, pt: (b, 0, 0)),
        scratch_shapes=[pltpu.VMEM((2, nfeat, npg, page), F32), pltpu.SemaphoreType.DMA((2,))])
    out = pl.pallas_call(
        functools.partial(_summ_paged_kernel, npg, layer),
        grid_spec=grid_spec,
        out_shape=jax.ShapeDtypeStruct((B, npg, wout), F32),
        compiler_params=_cp(("arbitrary",), 56),
        name="cmp_summaries_paged",
    )(page_table.reshape(-1), w_feat, view)
    return out.reshape(B, npg * (page // CMP_BLOCK), 128)


def _hgrn_kernel(C, nchunk, q_ref, k_ref, v_ref, g_ref, s0_ref, o_ref, st_ref, st_sc):
    t = pl.program_id(1)

    @pl.when(t == 0)
    def _():
        st_sc[...] = s0_ref[0]

    W = B_HEADS * B_DK
    head_mask = (_row((W, W)) // B_DV) == (_lane((W, W)) // B_DK)
    ones_blk = jnp.where(head_mask, 1.0, 0.0).astype(BF16)
    tri = jnp.where(_row((C, C)) >= _lane((C, C)), 1.0, 0.0)
    srow = _row((C, W))

    def chunk(c, carry):
        r0 = pl.multiple_of(c * C, C)
        q = q_ref[0, pl.ds(r0, C), :]
        k = k_ref[0, pl.ds(r0, C), :]
        v = v_ref[0, pl.ds(r0, C), :]
        g = g_ref[0, pl.ds(r0, C), :]
        G = lax.dot_general(tri, g, (((1,), (0,)), ((), ())), preferred_element_type=F32,
                            precision=lax.Precision.HIGHEST)
        st = st_sc[...]
        ps = []
        for tt in range(C):
            d = jnp.where(srow <= tt, G[tt:tt + 1, :] - G, NEG_INF)
            ps.append(q[tt:tt + 1, :] * jnp.exp(d) * k)
        P = jnp.concatenate(ps, axis=0)
        R = _mm(P.astype(BF16), ones_blk)
        o_intra = jnp.sum(R.reshape(C, C, W) * v[None, :, :], axis=1)
        o_inter = _nt((q * jnp.exp(G)).astype(BF16), st.astype(BF16))
        o_ref[0, pl.ds(r0, C), :] = o_intra + o_inter
        g_last = G[C - 1:C, :]
        upd = _tn(v.astype(BF16), (k * jnp.exp(g_last - G)).astype(BF16))
        st_sc[...] = st * jnp.exp(g_last) + jnp.where(head_mask, upd, 0.0)
        return carry

    lax.fori_loop(0, nchunk, chunk, 0, unroll=min(nchunk, 4))

    @pl.when(t == pl.num_programs(1) - 1)
    def _():
        st_ref[0] = st_sc[...]


def _hgrn(hq, hk, hv, hf, s0t, C, tr):
    B, L, W = hq.shape
    seq = pl.BlockSpec((1, tr, W), lambda b, t: (b, t, 0))
    st = pl.BlockSpec((1, W, W), lambda b, t: (b, 0, 0))
    return pl.pallas_call(
        functools.partial(_hgrn_kernel, C, tr // C),
        grid=(B, L // tr),
        in_specs=[seq, seq, seq, seq, st],
        out_specs=[seq, st],
        out_shape=[jax.ShapeDtypeStruct((B, L, W), F32), jax.ShapeDtypeStruct((B, W, W), F32)],
        scratch_shapes=[pltpu.VMEM((W, W), F32)],
        compiler_params=_cp(("arbitrary", "arbitrary")),
        name="hgrn2",
    )(hq, hk, hv, hf, s0t)


def _state_to_blockdiag_t(s):
    B = s.shape[0]
    eye = jnp.eye(B_HEADS, dtype=s.dtype)
    return jnp.einsum("bhkv,hg->bhvgk", s, eye).reshape(B, B_HEADS * B_DV, B_HEADS * B_DK)


def _blockdiag_t_to_state(st):
    B = st.shape[0]
    st5 = st.reshape(B, B_HEADS, B_DV, B_HEADS, B_DK)
    return jnp.stack([st5[:, h, :, h, :] for h in range(B_HEADS)], axis=1).transpose(0, 1, 3, 2)


def _diff_dec_kernel(G, page, lam_init, pt_ref, q_ref, kn_ref, vn_ref, dl_ref, *rest):
    kps, vps = rest[:G], rest[G:2 * G]
    o_ref = rest[2 * G]
    m_sc, l_sc, acc_sc = rest[2 * G + 1:]
    s_idx = pl.program_id(1)
    nq = q_ref.shape[1]

    @pl.when(s_idx == 0)
    def _():
        m_sc[...] = jnp.full(m_sc.shape, NEG_INF, F32)
        l_sc[...] = jnp.zeros(l_sc.shape, F32)
        acc_sc[...] = jnp.zeros(acc_sc.shape, F32)

    qparts = []
    for h in range(A_HEADS):
        qp = q_ref[0, :, 128 * h:128 * (h + 1)]
        lane = _lane(qp.shape)
        qparts += [jnp.where(lane < 64, qp, 0.0), jnp.where(lane >= 64, qp, 0.0)]
    qall = jnp.concatenate(qparts, axis=0).astype(BF16)
    rows = qall.shape[0]
    rpp = page * A_HEADS
    same_head = (_lane((rows, G * rpp)) % A_HEADS) == (_row((rows, G * rpp)) // (2 * nq))
    s = jnp.concatenate([_nt(qall, kps[g][...].astype(BF16)) for g in range(G)], axis=1)
    s = jnp.where(same_head, s, NEG_INF)
    p, alpha, m_next, l_next = _online(s, m_sc[:, :1], l_sc[:, :1])
    pb = p.astype(BF16)
    pv = _mm(pb[:, 0:rpp], vps[0][...].astype(BF16))
    for g in range(1, G):
        pv = pv + _mm(pb[:, rpp * g:rpp * (g + 1)], vps[g][...].astype(BF16))
    acc_sc[...] = acc_sc[...] * alpha + pv
    m_sc[...] = jnp.broadcast_to(m_next, m_sc.shape)
    l_sc[...] = jnp.broadcast_to(l_next, l_sc.shape)

    @pl.when(s_idx == pl.num_programs(1) - 1)
    def _():
        lam = _diff_lambda(dl_ref[...], lam_init)
        kn = jnp.concatenate([kn_ref[0, :, 128 * h:128 * (h + 1)] for h in range(A_HEADS)], axis=0)
        vn = jnp.concatenate([vn_ref[0, :, 128 * h:128 * (h + 1)] for h in range(A_HEADS)], axis=0)
        nk = A_HEADS * nq
        r, c = _row((rows, nk)), _lane((rows, nk))
        ok = jnp.logical_and(c // nq == r // (2 * nq), c % nq <= r % nq)
        sn = jnp.where(ok, _nt(qall, kn.astype(BF16)), NEG_INF)
        p2, alpha2, m2, l2 = _online(sn, m_sc[:, :1], l_sc[:, :1])
        o = (acc_sc[...] * alpha2 + _mm(p2.astype(BF16), vn.astype(BF16))) / l2
        for h in range(A_HEADS):
            o_ref[0, :, 128 * h:128 * (h + 1)] = (o[2 * nq * h:2 * nq * h + nq]
                                                  - lam * o[2 * nq * h + nq:2 * nq * (h + 1)])


def _diff_decode(aq, ak_new, av_new, dl, lam_init, pool_k, pool_v, layer, page_table, G):
    B, nq, _ = aq.shape
    depth, n_pool, page = pool_k.shape[:3]
    npg = page_table.shape[1]
    rpp = page * A_HEADS
    kview = pool_k.reshape(depth, n_pool, rpp, 128)
    vview = pool_v.reshape(depth, n_pool, rpp, 128)
    pt = page_table.reshape(-1)
    rows = A_HEADS * 2 * nq

    def page_spec(g):
        return pl.BlockSpec((None, None, rpp, 128), lambda b, s, pt: (layer, pt[b * npg + s * G + g], 0, 0))

    tok = pl.BlockSpec((1, nq, 512), lambda b, s, pt: (b, 0, 0))
    grid_spec = pltpu.PrefetchScalarGridSpec(
        num_scalar_prefetch=1, grid=(B, npg // G),
        in_specs=[tok, tok, tok, pl.BlockSpec(dl.shape, lambda b, s, pt: (0, 0))]
        + [page_spec(g) for g in range(G)] * 2,
        out_specs=tok,
        scratch_shapes=[pltpu.VMEM((rows, LANES), F32), pltpu.VMEM((rows, LANES), F32),
                        pltpu.VMEM((rows, 128), F32)])
    return pl.pallas_call(
        functools.partial(_diff_dec_kernel, G, page, lam_init),
        grid_spec=grid_spec,
        out_shape=jax.ShapeDtypeStruct((B, nq, 512), F32),
        compiler_params=_cp(("arbitrary", "arbitrary")),
        name="diff_decode",
    )(pt, aq, ak_new, av_new, dl, *([kview] * G), *([vview] * G))


def _sel_dec_kernel(G, page, pt_ref, q_ref, new_ref, msk_ref, *rest):
    pages = rest[:G]
    o_ref = rest[G]
    m_sc, l_sc, acc_sc = rest[G + 1:]
    s_idx = pl.program_id(1)
    nq = q_ref.shape[1]
    rows = C_HEADS * nq
    bpp = page // CMP_BLOCK

    @pl.when(s_idx == 0)
    def _():
        m_sc[...] = jnp.full(m_sc.shape, NEG_INF, F32)
        l_sc[...] = jnp.zeros(l_sc.shape, F32)
        acc_sc[...] = jnp.zeros(acc_sc.shape, F32)

    qall = _stack_low_q(q_ref[0])
    nb = msk_ref.shape[2]
    nkeys = G * page
    msk = msk_ref[0]
    msk4 = jnp.concatenate([msk] * C_HEADS, axis=0).astype(BF16)
    expand = jnp.where(_row((nb, nkeys)) == s_idx * (G * bpp) + _lane((nb, nkeys)) // CMP_BLOCK, 1.0, 0.0)
    valid = _mm(msk4, expand.astype(BF16)) > 0.5
    kvs = [pages[g][...].astype(BF16) for g in range(G)]
    s = jnp.concatenate([_mm(qall, kvs[g]) for g in range(G)], axis=1)
    s = jnp.where(valid, s, NEG_INF)
    p, alpha, m_next, l_next = _online(s, m_sc[:, :1], l_sc[:, :1])
    pb = p.astype(BF16)
    pv = _nt(pb[:, 0:page], kvs[0])
    for g in range(1, G):
        pv = pv + _nt(pb[:, page * g:page * (g + 1)], kvs[g])
    acc_sc[...] = acc_sc[...] * alpha + pv
    m_sc[...] = jnp.broadcast_to(m_next, m_sc.shape)
    l_sc[...] = jnp.broadcast_to(l_next, l_sc.shape)

    @pl.when(s_idx == pl.num_programs(1) - 1)
    def _():
        kvn = new_ref[0].astype(BF16)
        causal = _lane((rows, nq)) <= (_row((rows, nq)) % nq)
        sn = jnp.where(causal, _nt(qall, kvn), NEG_INF)
        p2, alpha2, m2, l2 = _online(sn, m_sc[:, :1], l_sc[:, :1])
        o = (acc_sc[...] * alpha2 + _mm(p2.astype(BF16), kvn)) / l2
        for g in range(2):
            o_ref[0, :, 128 * g:128 * (g + 1)] = _pair_out(o[2 * g * nq:(2 * g + 1) * nq],
                                                           o[(2 * g + 1) * nq:(2 * g + 2) * nq])


def _sel_decode(cqr, sel_new, selmask, pool, layer, page_table, G):
    B, nq, _ = cqr.shape
    depth, n_pool, page = pool.shape[:3]
    npg = page_table.shape[1]
    nb = selmask.shape[2]
    view = _feature_major(pool)
    pt = page_table.reshape(-1)
    rows = C_HEADS * nq

    def page_spec(g):
        return pl.BlockSpec((None, None, 128, page), lambda b, s, pt: (layer, pt[b * npg + s * G + g], 0, 0))

    grid_spec = pltpu.PrefetchScalarGridSpec(
        num_scalar_prefetch=1, grid=(B, npg // G),
        in_specs=[pl.BlockSpec((1, nq, 256), lambda b, s, pt: (b, 0, 0)),
                  pl.BlockSpec((1, nq, 128), lambda b, s, pt: (b, 0, 0)),
                  pl.BlockSpec((1, nq, nb), lambda b, s, pt: (b, 0, 0))] + [page_spec(g) for g in range(G)],
        out_specs=pl.BlockSpec((1, nq, 256), lambda b, s, pt: (b, 0, 0)),
        scratch_shapes=[pltpu.VMEM((rows, LANES), F32), pltpu.VMEM((rows, LANES), F32),
                        pltpu.VMEM((rows, 128), F32)])
    return pl.pallas_call(
        functools.partial(_sel_dec_kernel, G, page),
        grid_spec=grid_spec,
        out_shape=jax.ShapeDtypeStruct((B, nq, 256), F32),
        compiler_params=_cp(("arbitrary", "arbitrary")),
        name="sel_decode",
    )(pt, cqr, sel_new, selmask, *([view] * G))


def _win_dec_kernel(q_ref, buf_ref, new_ref, o_ref):
    nq = q_ref.shape[1]
    W = buf_ref.shape[2]
    rows = C_HEADS * nq
    qall = _stack_low_q(q_ref[0])
    kvb = buf_ref[0].astype(BF16)
    kvn = new_ref[0].astype(BF16)
    tok = _row((rows, W)) % nq
    sb = jnp.where(_lane((rows, W)) > tok, _mm(qall, kvb), NEG_INF)
    sn = jnp.where(_lane((rows, nq)) <= _row((rows, nq)) % nq, _nt(qall, kvn), NEG_INF)
    m = jnp.maximum(jnp.max(sb, axis=1, keepdims=True), jnp.max(sn, axis=1, keepdims=True))
    pb, pn = jnp.exp2(sb - m), jnp.exp2(sn - m)
    l = jnp.sum(pb, axis=1, keepdims=True) + jnp.sum(pn, axis=1, keepdims=True)
    o = (_nt(pb.astype(BF16), kvb) + _mm(pn.astype(BF16), kvn)) / l
    for g in range(2):
        o_ref[0, :, 128 * g:128 * (g + 1)] = _pair_out(o[2 * g * nq:(2 * g + 1) * nq],
                                                       o[(2 * g + 1) * nq:(2 * g + 2) * nq])


def _win_decode(cqr, win_buf, win_new):
    B, nq, _ = cqr.shape
    W = win_buf.shape[2]
    return pl.pallas_call(
        _win_dec_kernel,
        grid=(B,),
        in_specs=[pl.BlockSpec((1, nq, 256), lambda b: (b, 0, 0)),
                  pl.BlockSpec((1, 128, W), lambda b: (b, 0, 0)),
                  pl.BlockSpec((1, nq, 128), lambda b: (b, 0, 0))],
        out_specs=pl.BlockSpec((1, nq, 256), lambda b: (b, 0, 0)),
        out_shape=jax.ShapeDtypeStruct((B, nq, 256), F32),
        compiler_params=_cp(("arbitrary",)),
        name="win_decode",
    )(cqr, win_buf, win_new)


def _merge_kernel(lam_init, x_ref, g1_ref, ao_ref, bo_ref, hg_ref, oc_ref, os_ref, ow_ref, cg_ref,
                  sub_ref, hn_ref, w_ref, o_ref):
    tm = x_ref.shape[0]
    sub = sub_ref[...]
    y = jnp.zeros((tm, w_ref.shape[1]), F32)
    for h in range(A_HEADS):
        a = ao_ref[:, 128 * h:128 * (h + 1)]
        a = a * lax.rsqrt(jnp.mean(a * a, axis=-1, keepdims=True) + RMS_EPS) * sub * (1.0 - lam_init)
        y = y + _mm(a.astype(BF16), w_ref[128 * h:128 * (h + 1), :])
    lane = _lane((tm, 128))
    lo = lane < 64
    hn = hn_ref[...]
    for g in range(2):
        b = bo_ref[:, 128 * g:128 * (g + 1)]
        b2 = b * b
        s_lo = jnp.sum(jnp.where(lo, b2, 0.0), axis=-1, keepdims=True)
        s_hi = jnp.sum(jnp.where(lo, 0.0, b2), axis=-1, keepdims=True)
        ms = jnp.where(lo, s_lo, s_hi) * (1.0 / B_DV)
        b = b * lax.rsqrt(ms + RMS_EPS) * hn * _silu(hg_ref[:, 128 * g:128 * (g + 1)])
        y = y + _mm(b.astype(BF16), w_ref[512 + 128 * g:512 + 128 * (g + 1), :])
    sig = jax.nn.sigmoid(cg_ref[...])
    for g in range(2):
        c = jnp.zeros((tm, 128), F32)
        for br, ref in enumerate((oc_ref, os_ref, ow_ref)):
            ge = jnp.broadcast_to(sig[:, 6 * g + br:6 * g + br + 1], (tm, 128))
            go = jnp.broadcast_to(sig[:, 6 * g + 3 + br:6 * g + 3 + br + 1], (tm, 128))
            c = c + jnp.where(lo, ge, go) * ref[:, 128 * g:128 * (g + 1)]
        y = y + _mm(c.astype(BF16), w_ref[768 + 128 * g:768 + 128 * (g + 1), :])
    o_ref[...] = x_ref[...] + g1_ref[...] * y


def _merge(lam_init, x, g1, a_o, b_o, hg, o_cmp, o_sel, o_win, cg, sub, hn2, w_bf, tm):
    rows, d = x.shape
    mod_spec = (pl.BlockSpec((1, d), lambda i: (0, 0)) if g1.shape[0] == 1
                else pl.BlockSpec((tm, d), lambda i: (i, 0)))

    def rs(w):
        return pl.BlockSpec((tm, w), lambda i: (i, 0))

    def full(a):
        return pl.BlockSpec(a.shape, lambda i: (0, 0))

    return pl.pallas_call(
        functools.partial(_merge_kernel, lam_init),
        grid=(rows // tm,),
        in_specs=[rs(d), mod_spec, rs(512), rs(256), rs(256), rs(256), rs(256), rs(256), rs(12),
                  full(sub), full(hn2), full(w_bf)],
        out_specs=rs(d),
        out_shape=jax.ShapeDtypeStruct((rows, d), F32),
        compiler_params=_cp(("arbitrary",)),
        name="merge_out_proj",
    )(x, g1, a_o, b_o, hg, o_cmp, o_sel, o_win, cg, sub, hn2, w_bf)


def _ffn_kernel(fc, x_ref, sc_ref, sh_ref, g2_ref, nw_ref, wgu_ref, wd_ref, o_ref):
    x = x_ref[...]
    hb = _rms_mod(x, nw_ref[...], sc_ref[...], sh_ref[...]).astype(BF16)
    dff = wd_ref.shape[0]
    y = jnp.zeros(x.shape, F32)
    for c in range(dff // fc):
        g = _mm(hb, wgu_ref[:, c * fc:(c + 1) * fc])
        u = _mm(hb, wgu_ref[:, dff + c * fc:dff + (c + 1) * fc])
        y = y + _mm((_silu(g) * u).astype(BF16), wd_ref[c * fc:(c + 1) * fc, :])
    o_ref[...] = x + g2_ref[...] * y


def _ffn_dense(x, sc, sh, g2, nw, wgu_bf, wd_bf, tm, fc):
    rows, d = x.shape
    mod_spec = (pl.BlockSpec((1, d), lambda i: (0, 0)) if sc.shape[0] == 1
                else pl.BlockSpec((tm, d), lambda i: (i, 0)))
    return pl.pallas_call(
        functools.partial(_ffn_kernel, fc),
        grid=(rows // tm,),
        in_specs=[pl.BlockSpec((tm, d), lambda i: (i, 0)), mod_spec, mod_spec, mod_spec,
                  pl.BlockSpec((1, d), lambda i: (0, 0)),
                  pl.BlockSpec(wgu_bf.shape, lambda i: (0, 0)),
                  pl.BlockSpec(wd_bf.shape, lambda i: (0, 0))],
        out_specs=pl.BlockSpec((tm, d), lambda i: (i, 0)),
        out_shape=jax.ShapeDtypeStruct((rows, d), F32),
        compiler_params=_cp(("arbitrary",), 56),
        name="ffn_dense",
    )(x, sc, sh, g2, nw, wgu_bf, wd_bf)


def _router_kernel(x_ref, sc_ref, sh_ref, nw_ref, wr_ref, br_ref, gate_ref, pos_ref, cnt_ref):
    h = _rms_mod(x_ref[...], nw_ref[...], sc_ref[...], sh_ref[...])
    tm = h.shape[0]
    logits = lax.dot_general(wr_ref[...], h, (((1,), (1,)), ((), ())), preferred_element_type=F32,
                             precision=lax.Precision.HIGHEST) + br_ref[...]
    ne = logits.shape[0]
    row = _row((ne, tm)).astype(F32)
    m1 = jnp.max(logits, axis=0, keepdims=True)
    i1 = jnp.min(jnp.where(logits == m1, row, float(ne)), axis=0, keepdims=True)
    hit1 = row == i1
    l2 = jnp.where(hit1, -3.0e38, logits)
    m2 = jnp.max(l2, axis=0, keepdims=True)
    i2 = jnp.min(jnp.where(l2 == m2, row, float(ne)), axis=0, keepdims=True)
    hit2 = row == i2
    e2 = jnp.exp(m2 - m1)
    den = 1.0 + e2
    gate_ref[...] = jnp.where(hit1, 1.0 / den, jnp.where(hit2, e2 / den, 0.0))
    member = jnp.logical_or(hit1, hit2)
    mem = jnp.where(member, 1.0, 0.0)
    before = jnp.where(_row((tm, tm)) < _lane((tm, tm)), 1.0, 0.0).astype(BF16)
    pos = _mm(mem.astype(BF16), before)
    pos_ref[...] = jnp.where(member, pos, -1.0)
    cnt_ref[0] = jnp.broadcast_to(jnp.sum(mem, axis=1, keepdims=True), (ne, LANES)).astype(jnp.int32)


def _router(x, sc, sh, nw, w_router, b_router, tm):
    rows, d = x.shape
    ne = w_router.shape[1]
    nt = rows // tm
    mod_spec = (pl.BlockSpec((1, d), lambda i: (0, 0)) if sc.shape[0] == 1
                else pl.BlockSpec((tm, d), lambda i: (i, 0)))
    return pl.pallas_call(
        _router_kernel,
        grid=(nt,),
        in_specs=[pl.BlockSpec((tm, d), lambda i: (i, 0)), mod_spec, mod_spec,
                  pl.BlockSpec((1, d), lambda i: (0, 0)),
                  pl.BlockSpec((ne, d), lambda i: (0, 0)),
                  pl.BlockSpec((ne, 1), lambda i: (0, 0))],
        out_specs=[pl.BlockSpec((ne, tm), lambda i: (0, i)), pl.BlockSpec((ne, tm), lambda i: (0, i)),
                   pl.BlockSpec((1, ne, LANES), lambda i: (i, 0, 0))],
        out_shape=[jax.ShapeDtypeStruct((ne, rows), F32), jax.ShapeDtypeStruct((ne, rows), F32),
                   jax.ShapeDtypeStruct((nt, ne, LANES), jnp.int32)],
        compiler_params=_cp(("arbitrary",)),
        name="moe_router",
    )(x, sc, sh, nw, w_router.T, b_router.reshape(ne, 1))


def _moe_kernel(nf, final_norm, cnt_ref, x_ref, sc_ref, sh_ref, g2_ref, nw_ref, nfin_ref, gate_ref, pos_ref,
                wg_ref, wu_ref, wd_ref, o_ref, hb_sc, acc_sc, xs_sc, ys_sc):
    i, e, f = pl.program_id(0), pl.program_id(1), pl.program_id(2)
    ne = pl.num_programs(1)
    tm = x_ref.shape[0]
    nsub = (cnt_ref[i * ne + e] + (MOE_SUB - 1)) // MOE_SUB

    @pl.when(jnp.logical_and(e == 0, f == 0))
    def _():
        hb_sc[...] = _rms_mod(x_ref[...], nw_ref[...], sc_ref[...], sh_ref[...]).astype(BF16)
        acc_sc[...] = jnp.zeros(acc_sc.shape, F32)

    def onehot(j):
        pos_row = pos_ref[pl.ds(e, 1), :]
        slot = (_row((MOE_SUB, tm)) + j * MOE_SUB).astype(F32)
        return pos_row == slot

    def body(j, carry):
        r0 = pl.multiple_of(j * MOE_SUB, MOE_SUB)

        @pl.when(f == 0)
        def _():
            sel = jnp.where(onehot(j), 1.0, 0.0).astype(BF16)
            xs_sc[pl.ds(r0, MOE_SUB), :] = _mm(sel, hb_sc[...]).astype(BF16)
            ys_sc[pl.ds(r0, MOE_SUB), :] = jnp.zeros((MOE_SUB, ys_sc.shape[1]), F32)

        xs = xs_sc[pl.ds(r0, MOE_SUB), :]
        act = (_silu(_mm(xs, wg_ref[0])) * _mm(xs, wu_ref[0])).astype(BF16)
        ys_sc[pl.ds(r0, MOE_SUB), :] += _mm(act, wd_ref[0])

        @pl.when(f == nf - 1)
        def _():
            oh = onehot(j)
            gate_row = gate_ref[pl.ds(e, 1), :]
            gr = jnp.sum(jnp.where(oh, gate_row, 0.0), axis=1, keepdims=True)
            y = ys_sc[pl.ds(r0, MOE_SUB), :] * gr
            y_hi = y.astype(BF16)
            y_lo = (y - y_hi.astype(F32)).astype(BF16)
            sel = jnp.where(oh, 1.0, 0.0).astype(BF16)
            acc_sc[...] += _tn(sel, y_hi) + _tn(sel, y_lo)
        return carry

    lax.fori_loop(0, nsub, body, 0)

    @pl.when(jnp.logical_and(e == ne - 1, f == nf - 1))
    def _():
        xo = x_ref[...] + g2_ref[...] * acc_sc[...]
        if final_norm:
            xo = xo * lax.rsqrt(jnp.mean(xo * xo, axis=-1, keepdims=True) + RMS_EPS) * nfin_ref[...]
        o_ref[...] = xo


def _moe(x, sc, sh, g2, nw, nfin, gate_t, pos_t, counts, wgu_bf, wd_bf, tm, nf, final_norm):
    rows, d = x.shape
    ne, _, dff2 = wgu_bf.shape
    dff = dff2 // 2
    fc = dff // nf
    nt = rows // tm
    mod_spec = (pl.BlockSpec((1, d), lambda i, e, f, c: (0, 0)) if sc.shape[0] == 1
                else pl.BlockSpec((tm, d), lambda i, e, f, c: (i, 0)))
    vec = pl.BlockSpec((1, d), lambda i, e, f, c: (0, 0))
    route = pl.BlockSpec((ne, tm), lambda i, e, f, c: (0, i))
    grid_spec = pltpu.PrefetchScalarGridSpec(
        num_scalar_prefetch=1, grid=(nt, ne, nf),
        in_specs=[pl.BlockSpec((tm, d), lambda i, e, f, c: (i, 0)), mod_spec, mod_spec, mod_spec, vec, vec,
                  route, route,
                  pl.BlockSpec((1, d, fc), lambda i, e, f, c: (e, 0, f)),
                  pl.BlockSpec((1, d, fc), lambda i, e, f, c: (e, 0, nf + f)),
                  pl.BlockSpec((1, fc, d), lambda i, e, f, c: (e, f, 0))],
        out_specs=pl.BlockSpec((tm, d), lambda i, e, f, c: (i, 0)),
        scratch_shapes=[pltpu.VMEM((tm, d), BF16), pltpu.VMEM((tm, d), F32),
                        pltpu.VMEM((tm, d), BF16), pltpu.VMEM((tm, d), F32)])
    return pl.pallas_call(
        functools.partial(_moe_kernel, nf, final_norm),
        grid_spec=grid_spec,
        out_shape=jax.ShapeDtypeStruct((rows, d), F32),
        compiler_params=_cp(("arbitrary", "arbitrary", "arbitrary"), 56),
        name="moe_experts",
    )(counts, x, sc, sh, g2, nw, nfin, gate_t, pos_t, wgu_bf, wgu_bf, wd_bf)


def _norm_kernel(x_ref, w_ref, o_ref):
    x = x_ref[...]
    o_ref[...] = x * lax.rsqrt(jnp.mean(x * x, axis=-1, keepdims=True) + RMS_EPS) * w_ref[...]


def _final_norm(x, w, tm):
    rows, d = x.shape
    return pl.pallas_call(
        _norm_kernel, grid=(rows // tm,),
        in_specs=[pl.BlockSpec((tm, d), lambda i: (i, 0)), pl.BlockSpec((1, d), lambda i: (0, 0))],
        out_specs=pl.BlockSpec((tm, d), lambda i: (i, 0)),
        out_shape=jax.ShapeDtypeStruct((rows, d), F32),
        compiler_params=_cp(("arbitrary",)),
        name="final_norm",
    )(x, w)


def _rope_tables(pos):
    inv = ROPE_THETA ** (-jnp.arange(0, 64, 2, dtype=F32) / 64)
    ang = pos.astype(F32)[:, None] * inv[None, :]
    cos, sin = jnp.cos(ang), jnp.sin(ang)
    return jnp.tile(cos, (1, 4)), jnp.tile(jnp.concatenate([-sin, sin], axis=1), (1, 2))


def _pick(n, pref):
    t = min(n, pref)
    while n % t:
        t //= 2
    return t


def kernel(x_prompt, x_sample, cache_diff_k, cache_diff_v, cache_nsa_cmp, cache_nsa_sel, state_nsa_win, state_hgrn, page_table, c_prompt, c_sample, w_in, w_out, w_cmp, diff_lambda, diff_subln, hgrn_lb_logits, hgrn_norm, norm_mix, norm_ffn, norm_final, w_ada, b_ada, ffn_w_gate_up, ffn_w_down, moe_w_router, moe_b_router, moe_w_gate_up, moe_w_down):
    depth = w_in.shape[0]
    _, L, D = x_prompt.shape
    Bs, Ls, _ = x_sample.shape
    page = cache_diff_k.shape[2]
    past = page_table.shape[1] * page
    rows_s = Bs * Ls

    kvc_sample = [_summaries_paged(cache_nsa_cmp, l, page_table,
                                   _cmp_weight_by_feature(w_cmp[l], page // CMP_BLOCK)) for l in range(depth)]

    n_c = 1 + Bs
    n_cp = -(-n_c // 8) * 8
    c_all = jnp.concatenate([c_prompt, c_sample, jnp.zeros((n_cp - n_c, D), F32)], axis=0)
    mod = _ada(c_all, w_ada, b_ada)

    cos_p, sin_p = _rope_tables(jnp.arange(L))
    cos_s, sin_s = _rope_tables(past + (jnp.arange(rows_s) % Ls))

    xp = x_prompt.reshape(L, D)
    xs = x_sample.reshape(rows_s, D)
    tmp = _pick(L, 512)
    tq_a, tk_a = _pick(L, 256), _pick(L, 1024)
    tq_s, tk_s = _pick(L, 256), _pick(L, 1024)
    G_diff = _pick(page_table.shape[1], 16)
    G_sel = _pick(page_table.shape[1], 32)
    outs_p = [[] for _ in range(6)]
    outs_s = [[] for _ in range(6)]

    for l in range(depth):
        lam_init = 0.8 - 0.6 * math.exp(-0.3 * l)
        w_in_bf = w_in[l].astype(BF16)
        w_out_bf = w_out[l].astype(BF16)
        wc = _cmp_weight(w_cmp[l])
        dl = diff_lambda[l]
        sub = diff_subln[l].reshape(1, A_DV)
        hn2 = jnp.tile(hgrn_norm[l].reshape(1, B_DV), (1, 2))
        nmix = norm_mix[l].reshape(1, D)
        nffn = norm_ffn[l].reshape(1, D)
        mp = [m for m in jnp.split(mod[l, 0:1], 6, axis=-1)]
        ms = [jnp.repeat(m, Ls, axis=0) for m in jnp.split(mod[l, 1:1 + Bs], 6, axis=-1)]

        (aq, ak, av, hq, hk, hf, hv, hg, cq, cqr, cmpr, sel, win, cg) = _proj(
            l, xp, mp[1], mp[0], nmix, w_in_bf, cos_p, sin_p, hgrn_lb_logits, tmp)
        a_o = _diff_flash(aq, ak, av, dl, lam_init, tq_a, tk_a, A_HEADS)
        s0 = jnp.zeros((1, B_HEADS * B_DV, B_HEADS * B_DK), F32)
        b_o, st = _hgrn(hq[None], hk[None], hv[None], hf[None], s0, 16, _pick(L, 512))
        kvc = _summaries_prompt(cmpr, wc)
        o_cmp, selmask = _cmp_attend(cq[None], kvc[None], 0, TOP_N, _pick(L, 256))
        o_sel = _sel_flash(cqr, sel, selmask[0], tq_s, tk_s)
        o_win = _win_prompt(cqr, win, _pick(L, WINDOW))
        xp = _merge(lam_init, xp, mp[2], a_o, b_o[0], hg, o_cmp[0], o_sel, o_win, cg, sub, hn2, w_out_bf, tmp)
        W = min(WINDOW, L)
        for i, a in enumerate((ak.reshape(1, L, A_HEADS, 2 * A_DK), av.reshape(1, L, A_HEADS, A_DV),
                               cmpr.reshape(1, L, 2, C_DH), sel.reshape(1, L, 2, C_DH),
                               win[L - W:].reshape(1, W, 2, C_DH), _blockdiag_t_to_state(st))):
            outs_p[i].append(a)

        (aq, ak, av, hq, hk, hf, hv, hg, cq, cqr, cmpr, sel, win, cg) = _proj(
            l, xs, ms[1], ms[0], nmix, w_in_bf, cos_s, sin_s, hgrn_lb_logits, rows_s)
        b3 = lambda a: a.reshape(Bs, Ls, a.shape[-1])
        a_o = _diff_decode(b3(aq).astype(F32), b3(ak), b3(av), dl, lam_init, cache_diff_k, cache_diff_v, l,
                           page_table, G_diff)
        b_o, st = _hgrn(b3(hq), b3(hk), b3(hv), b3(hf), _state_to_blockdiag_t(state_hgrn[l]), Ls, Ls)
        o_cmp, selmask = _cmp_attend(b3(cq).astype(F32), kvc_sample[l], past, TOP_N - 1, Ls)
        o_sel = _sel_decode(b3(cqr).astype(F32), b3(sel), selmask, cache_nsa_sel, l, page_table, G_sel)
        o_win = _win_decode(b3(cqr).astype(F32), _feature_major(state_nsa_win[l]), b3(win))
        xs = _merge(lam_init, xs, ms[2], a_o.reshape(rows_s, 512), b_o.reshape(rows_s, 256), hg,
                    o_cmp.reshape(rows_s, 256), o_sel.reshape(rows_s, 256), o_win.reshape(rows_s, 256), cg,
                    sub, hn2, w_out_bf, rows_s)
        win_all = jnp.concatenate([state_nsa_win[l], win.reshape(Bs, Ls, 2, C_DH)], axis=1)[:, Ls:]
        for i, a in enumerate((ak.reshape(Bs, Ls, A_HEADS, 2 * A_DK), av.reshape(Bs, Ls, A_HEADS, A_DV),
                               cmpr.reshape(Bs, Ls, 2, C_DH), sel.reshape(Bs, Ls, 2, C_DH),
                               win_all, _blockdiag_t_to_state(st))):
            outs_s[i].append(a)

        j = l // 2
        last = l == depth - 1
        nfin = norm_final.reshape(1, D)
        if l % 2 == 0:
            wgu = ffn_w_gate_up[j].astype(BF16)
            wd = ffn_w_down[j].astype(BF16)
            fc = wd.shape[0] // 2
            xp = _ffn_dense(xp, mp[4], mp[3], mp[5], nffn, wgu, wd, tmp, fc)
            xs = _ffn_dense(xs, ms[4], ms[3], ms[5], nffn, wgu, wd, rows_s, fc)
            if last:
                xp = _final_norm(xp, nfin, tmp)
                xs = _final_norm(xs, nfin, rows_s)
        else:
            wgu = moe_w_gate_up[j].astype(BF16)
            wd = moe_w_down[j].astype(BF16)
            for which in range(2):
                x, m6, tm = (xp, mp, _pick(L, 1024)) if which == 0 else (xs, ms, rows_s)
                gate_t, pos_t, cnt = _router(x, m6[4], m6[3], nffn, moe_w_router[j], moe_b_router[j], tm)
                y = _moe(x, m6[4], m6[3], m6[5], nffn, nfin, gate_t, pos_t, cnt[:, :, 0].reshape(-1),
                         wgu, wd, tm, 2, last)
                if which == 0:
                    xp = y
                else:
                    xs = y
            if not last:
                pass

    y_prompt = xp.reshape(1, L, D)
    y_sample = xs.reshape(Bs, Ls, D)
    p_st = [jnp.stack(a, axis=0) for a in outs_p]
    s_st = [jnp.stack(a, axis=0) for a in outs_s]
    return (y_prompt, y_sample, *p_st, *s_st)
```

```python
import functools
import math

import numpy as np
import jax
import jax.numpy as jnp
from jax import lax
from jax.experimental import pallas as pl
from jax.experimental.pallas import tpu as pltpu

F32 = jnp.float32
BF16 = jnp.bfloat16

A_HEADS = 4
A_DK = 64
A_DV = 128
B_HEADS = 4
B_DK = 64
B_DV = 64
C_HEADS = 4
C_DH = 64
CMP_BLOCK = 64
TOP_N = 16
WINDOW = 512
ROPE_THETA = 10000.0
N_EXPERTS = 8
RMS_EPS = 1e-6
NEG_INF = -1e30
FORCED_SCORE = 1e4
F_FLOOR = 1e-20
QK_SCALE = 0.125 * math.log2(math.e)
LANES = 128
MOE_SUB = 128
MOE_CAP = 2

_SPL = (512, 512, 512, 256, 256, 256, 256, 256, 128, 128, 128, 12)
_OFF = tuple(int(v) for v in np.cumsum((0,) + _SPL))
IN_COLS = _OFF[-1]


def _cp(sem, vmem_mb=48):
    return pltpu.CompilerParams(dimension_semantics=sem, vmem_limit_bytes=vmem_mb * 1024 * 1024)


def _silu(x):
    return x * jax.nn.sigmoid(x)


def _nt(a, b):
    return lax.dot_general(a, b, (((1,), (1,)), ((), ())), preferred_element_type=F32)


def _tn(a, b):
    return lax.dot_general(a, b, (((0,), (0,)), ((), ())), preferred_element_type=F32)


def _mm(a, b):
    return jnp.dot(a, b, preferred_element_type=F32)


def _lane(shape):
    return lax.broadcasted_iota(jnp.int32, shape, 1)


def _row(shape):
    return lax.broadcasted_iota(jnp.int32, shape, 0)


def _rms_mod(x, nw, sc, sh):
    h = x * lax.rsqrt(jnp.mean(x * x, axis=-1, keepdims=True) + RMS_EPS) * nw
    return h * (1.0 + sc) + sh


def _ada_kernel(c_ref, w_ref, b_ref, o_ref):
    s = _silu(c_ref[...]).astype(BF16)
    o_ref[0] = _mm(s, w_ref[0].astype(BF16)) + b_ref[0]


def _ada(c_all, w_ada, b_ada):
    depth, d, n = w_ada.shape
    r = c_all.shape[0]
    tn = 1536
    return pl.pallas_call(
        _ada_kernel,
        grid=(depth, n // tn),
        in_specs=[pl.BlockSpec((r, d), lambda l, j: (0, 0)),
                  pl.BlockSpec((1, d, tn), lambda l, j: (l, 0, j)),
                  pl.BlockSpec((1, 1, tn), lambda l, j: (l, 0, j))],
        out_specs=pl.BlockSpec((1, r, tn), lambda l, j: (l, 0, j)),
        out_shape=jax.ShapeDtypeStruct((depth, r, n), F32),
        compiler_params=_cp(("arbitrary", "arbitrary")),
        name="ada_mod",
    )(c_all, w_ada, b_ada.reshape(depth, 1, n))


def _proj_kernel(layer, x_ref, sc_ref, sh_ref, nw_ref, w_ref, cos_ref, sin_ref, lbl_ref,
                 aq_ref, ak_ref, av_ref, hq_ref, hk_ref, hf_ref, hv_ref, hg_ref,
                 cq_ref, cqr_ref, cmp_ref, sel_ref, win_ref, cg_ref):
    hb = _rms_mod(x_ref[...], nw_ref[...], sc_ref[...], sh_ref[...]).astype(BF16)
    cos = cos_ref[...]
    sin = sin_ref[...]
    lane = _lane(cos.shape)
    first = (lane % 64) < 32

    def mm(i):
        return _mm(hb, w_ref[:, _OFF[i]:_OFF[i + 1]])

    def rope(y, c, s):
        rot = jnp.where(first, pltpu.roll(y, 96, 1), pltpu.roll(y, 32, 1))
        return y * c + rot * s

    y = mm(0)
    for g in range(4):
        aq_ref[:, 128 * g:128 * (g + 1)] = (rope(y[:, 128 * g:128 * (g + 1)], cos, sin) * QK_SCALE).astype(BF16)
    y = mm(1)
    for g in range(4):
        ak_ref[:, 128 * g:128 * (g + 1)] = rope(y[:, 128 * g:128 * (g + 1)], cos, sin)
    av_ref[...] = mm(2)
    hq_ref[...] = _silu(mm(3))
    lbl = lbl_ref[...]
    e = jnp.exp(lbl - jnp.max(lbl, axis=0, keepdims=True))
    sm = e / jnp.sum(e, axis=0, keepdims=True)
    lb = jnp.sum(sm[0:layer + 1], axis=0, keepdims=True) - sm[0:1]
    f = lb + (1.0 - lb) * jax.nn.sigmoid(mm(4))
    hf_ref[...] = jnp.log(jnp.maximum(f, F_FLOOR))
    hk_ref[...] = 1.0 - f
    hv_ref[...] = mm(5)
    hg_ref[...] = mm(6)
    y = mm(7)
    cq_ref[...] = (y * QK_SCALE).astype(BF16)
    for g in range(2):
        cqr_ref[:, 128 * g:128 * (g + 1)] = (rope(y[:, 128 * g:128 * (g + 1)], cos, sin) * QK_SCALE).astype(BF16)
    cmp_ref[...] = mm(8)
    cos_k = jnp.where(lane < 64, cos, 1.0)
    sin_k = jnp.where(lane < 64, sin, 0.0)
    sel_ref[...] = rope(mm(9), cos_k, sin_k)
    win_ref[...] = rope(mm(10), cos_k, sin_k)
    cg_ref[...] = mm(11)


def _proj(layer, x, sc, sh, nw, w_bf, cos, sin, lbl, tm):
    rows, d = x.shape
    mrows = sc.shape[0]
    mod_spec = (pl.BlockSpec((1, d), lambda i: (0, 0)) if mrows == 1
                else pl.BlockSpec((tm, d), lambda i: (i, 0)))
    widths = (512, 512, 512, 256, 256, 256, 256, 256, 256, 256, 128, 128, 128, 12)
    dtypes = (BF16, F32, F32, F32, F32, F32, F32, F32, BF16, BF16, F32, F32, F32, F32)
    return pl.pallas_call(
        functools.partial(_proj_kernel, layer),
        grid=(rows // tm,),
        in_specs=[pl.BlockSpec((tm, d), lambda i: (i, 0)), mod_spec, mod_spec,
                  pl.BlockSpec((1, d), lambda i: (0, 0)),
                  pl.BlockSpec((d, IN_COLS), lambda i: (0, 0)),
                  pl.BlockSpec((tm, LANES), lambda i: (i, 0)),
                  pl.BlockSpec((tm, LANES), lambda i: (i, 0)),
                  pl.BlockSpec(lbl.shape, lambda i: (0, 0))],
        out_specs=[pl.BlockSpec((tm, w), lambda i: (i, 0)) for w in widths],
        out_shape=[jax.ShapeDtypeStruct((rows, w), dt) for w, dt in zip(widths, dtypes)],
        compiler_params=_cp(("arbitrary",), 56),
        name="in_proj",
    )(x, sc, sh, nw, w_bf, cos, sin, lbl)


def _online(s, m_prev, l_prev):
    m_next = jnp.maximum(m_prev, jnp.max(s, axis=1, keepdims=True))
    p = jnp.exp2(s - m_next)
    alpha = jnp.exp2(m_prev - m_next)
    l_next = alpha * l_prev + jnp.sum(p, axis=1, keepdims=True)
    return p, alpha, m_next, l_next


def _flash_step(s, valid, m_prev, rhs_aug):
    n = s.shape[1]
    chunks = [s[:, LANES * c:LANES * (c + 1)] for c in range(n // LANES)]
    if valid is not None:
        chunks = [jnp.where(valid[:, LANES * c:LANES * (c + 1)], ch, NEG_INF) for c, ch in enumerate(chunks)]
    mx = chunks[0]
    for ch in chunks[1:]:
        mx = jnp.maximum(mx, ch)
    m_next = jnp.maximum(m_prev, jnp.max(mx, axis=1, keepdims=True))
    p = jnp.concatenate([jnp.exp2(ch - m_next).astype(BF16) for ch in chunks], axis=1)
    return _mm(p, rhs_aug), m_next, jnp.exp2(m_prev - m_next)


def _diff_lambda(dl, lam_init):
    a = jnp.sum(dl[0:1] * dl[1:2], axis=1, keepdims=True)
    b = jnp.sum(dl[2:3] * dl[3:4], axis=1, keepdims=True)
    return jnp.exp(a) - jnp.exp(b) + lam_init


def _causal_pairs(nq, tq, tk):
    ii, jj = [], []
    for i in range(nq):
        for j in range(((i + 1) * tq - 1) // tk + 1):
            ii.append(i)
            jj.append(j)
    return jnp.asarray(ii, jnp.int32), jnp.asarray(jj, jnp.int32)


def _diff_flash_kernel(tq, tk, lam_init, ii_ref, jj_ref, q_ref, k_ref, v_ref, dl_ref, o_ref,
                       m_sc, l_sc, acc_sc):
    s_idx = pl.program_id(1)
    i = ii_ref[s_idx]
    j = jj_ref[s_idx]
    hpg = q_ref.shape[1] // 128

    @pl.when(j == 0)
    def _():
        m_sc[...] = jnp.full(m_sc.shape, NEG_INF, F32)
        l_sc[...] = jnp.zeros(l_sc.shape, F32)
        acc_sc[...] = jnp.zeros(acc_sc.shape, F32)

    ones = jnp.ones((tk, LANES), BF16)
    low = _lane((tq, 128)) < 64

    def step(masked):
        valid = (j * tk + _lane((tq, tk))) <= (i * tq + _row((tq, tk))) if masked else None
        for h in range(hpg):
            q = q_ref[:, 128 * h:128 * (h + 1)]
            kb = k_ref[:, 128 * h:128 * (h + 1)].astype(BF16)
            v_aug = jnp.concatenate([v_ref[:, 128 * h:128 * (h + 1)].astype(BF16), ones], axis=1)
            for mi in range(2):
                c = 2 * h + mi
                qm = jnp.where(low if mi == 0 else jnp.logical_not(low), q, jnp.zeros_like(q))
                pv, m_next, alpha = _flash_step(_nt(qm, kb), valid, m_sc[c], v_aug)
                acc_sc[c] = acc_sc[c] * alpha + pv[:, :128]
                l_sc[c] = l_sc[c] * alpha + pv[:, 128:]
                m_sc[c] = m_next

    needs_mask = (j * tk + tk - 1) > (i * tq)
    pl.when(needs_mask)(lambda: step(True))
    pl.when(jnp.logical_not(needs_mask))(lambda: step(False))

    @pl.when(j == ((i + 1) * tq - 1) // tk)
    def _():
        lam = _diff_lambda(dl_ref[...], lam_init)
        for h in range(hpg):
            o_ref[:, 128 * h:128 * (h + 1)] = (acc_sc[2 * h] / l_sc[2 * h]
                                               - lam * (acc_sc[2 * h + 1] / l_sc[2 * h + 1]))


def _diff_flash(aq, ak, av, dl, lam_init, tq, tk, hpg):
    L = aq.shape[0]
    nq = L // tq
    ii, jj = _causal_pairs(nq, tq, tk)
    w = 128 * hpg
    grid_spec = pltpu.PrefetchScalarGridSpec(
        num_scalar_prefetch=2,
        grid=(A_HEADS // hpg, int(ii.shape[0])),
        in_specs=[pl.BlockSpec((tq, w), lambda h, s, ii, jj: (ii[s], h)),
                  pl.BlockSpec((tk, w), lambda h, s, ii, jj: (jj[s], h)),
                  pl.BlockSpec((tk, w), lambda h, s, ii, jj: (jj[s], h)),
                  pl.BlockSpec(dl.shape, lambda h, s, ii, jj: (0, 0))],
        out_specs=pl.BlockSpec((tq, w), lambda h, s, ii, jj: (ii[s], h)),
        scratch_shapes=[pltpu.VMEM((2 * hpg, tq, LANES), F32), pltpu.VMEM((2 * hpg, tq, LANES), F32),
                        pltpu.VMEM((2 * hpg, tq, 128), F32)])
    return pl.pallas_call(
        functools.partial(_diff_flash_kernel, tq, tk, lam_init),
        grid_spec=grid_spec,
        out_shape=jax.ShapeDtypeStruct((L, 512), F32),
        compiler_params=_cp(("arbitrary", "arbitrary")),
        name="diff_flash",
    )(ii, jj, aq, ak, av, dl)


def _kk(kv):
    return jnp.where(_lane(kv.shape) < 64, kv, pltpu.roll(kv, 64, 1)).astype(BF16)


def _head_q(q_ref_or_val, h):
    qp = q_ref_or_val[:, 128 * (h // 2):128 * (h // 2 + 1)]
    lane = _lane(qp.shape)
    keep = (lane < 64) if h % 2 == 0 else (lane >= 64)
    return jnp.where(keep, qp, jnp.zeros_like(qp))


def _head_qb(q_ref_or_val, h):
    return _head_q(q_ref_or_val, h).astype(BF16)


def _pair_out(o_even, o_odd):
    return jnp.where(_lane(o_even.shape) < 64, pltpu.roll(o_even, 64, 1), o_odd)


def _stack_low_q(q):
    parts = []
    for g in range(C_HEADS // 2):
        qp = q[:, 128 * g:128 * (g + 1)]
        low = _lane(qp.shape) < 64
        parts += [jnp.where(low, qp, 0.0), jnp.where(low, pltpu.roll(qp, 64, 1), 0.0)]
    return jnp.concatenate(parts, axis=0).astype(BF16)


def _feature_major(a):
    n = a.ndim
    t = jnp.transpose(a, tuple(range(n - 3)) + (n - 2, n - 1, n - 3))
    return t.reshape(a.shape[:-3] + (2 * C_DH, a.shape[-3]))


def _sel_flash_kernel(tq, tk, ii_ref, jj_ref, q_ref, kv_ref, msk_ref, o_ref, m_sc, l_sc, acc_sc):
    s_idx = pl.program_id(0)
    i = ii_ref[s_idx]
    j = jj_ref[s_idx]
    bpt = tk // CMP_BLOCK

    @pl.when(j == 0)
    def _():
        m_sc[...] = jnp.full(m_sc.shape, NEG_INF, F32)
        l_sc[...] = jnp.zeros(l_sc.shape, F32)
        acc_sc[...] = jnp.zeros(acc_sc.shape, F32)

    kv = kv_ref[...]
    kv_aug = jnp.concatenate([kv.astype(BF16), jnp.ones((tk, LANES), BF16)], axis=1)
    kk = _kk(kv)
    nbl = msk_ref.shape[1]
    boff = (j * bpt) % nbl
    expand = jnp.where(_row((nbl, tk)) == boff + _lane((nbl, tk)) // CMP_BLOCK, 1.0, 0.0).astype(BF16)
    sel = _mm(msk_ref[...].astype(BF16), expand) > 0.5

    def step(masked):
        valid = sel
        if masked:
            valid = jnp.logical_and(sel, (j * tk + _lane((tq, tk))) <= (i * tq + _row((tq, tk))))
        for h in range(C_HEADS):
            pv, m_next, alpha = _flash_step(_nt(_head_qb(q_ref, h), kk), valid, m_sc[h], kv_aug)
            acc_sc[h] = acc_sc[h] * alpha + pv[:, :128]
            l_sc[h] = l_sc[h] * alpha + pv[:, 128:]
            m_sc[h] = m_next

    needs_mask = (j * tk + tk - 1) > (i * tq)
    pl.when(needs_mask)(lambda: step(True))
    pl.when(jnp.logical_not(needs_mask))(lambda: step(False))

    @pl.when(j == ((i + 1) * tq - 1) // tk)
    def _():
        for g in range(2):
            o_ref[:, 128 * g:128 * (g + 1)] = _pair_out(acc_sc[2 * g] / l_sc[2 * g],
                                                        acc_sc[2 * g + 1] / l_sc[2 * g + 1])


def _sel_flash(cqr, sel, selmask, tq, tk):
    L = cqr.shape[0]
    nb = selmask.shape[1]
    nbl = min(nb, LANES)
    bpt = tk // CMP_BLOCK
    ii, jj = _causal_pairs(L // tq, tq, tk)
    grid_spec = pltpu.PrefetchScalarGridSpec(
        num_scalar_prefetch=2,
        grid=(int(ii.shape[0]),),
        in_specs=[pl.BlockSpec((tq, 256), lambda s, ii, jj: (ii[s], 0)),
                  pl.BlockSpec((tk, 128), lambda s, ii, jj: (jj[s], 0)),
                  pl.BlockSpec((tq, nbl), lambda s, ii, jj: (ii[s], (jj[s] * bpt) // nbl))],
        out_specs=pl.BlockSpec((tq, 256), lambda s, ii, jj: (ii[s], 0)),
        scratch_shapes=[pltpu.VMEM((4, tq, LANES), F32), pltpu.VMEM((4, tq, LANES), F32),
                        pltpu.VMEM((4, tq, 128), F32)])
    return pl.pallas_call(
        functools.partial(_sel_flash_kernel, tq, tk),
        grid_spec=grid_spec,
        out_shape=jax.ShapeDtypeStruct((L, 256), F32),
        compiler_params=_cp(("arbitrary",)),
        name="sel_flash",
    )(ii, jj, cqr, sel, selmask)


def _win_prompt_kernel(t, q_ref, kp_ref, kc_ref, o_ref):
    i = pl.program_id(0)
    kvp = kp_ref[...]
    kvc = kc_ref[...]
    kkp, kkc = _kk(kvp), _kk(kvc)
    r, c = _row((t, t)), _lane((t, t))
    vp = jnp.logical_and(c > r, i > 0)
    vc = c <= r
    outs = []
    for h in range(C_HEADS):
        qh = _head_qb(q_ref, h)
        sp =jnp.where(vp, _nt(qh, kkp), NEG_INF)
        sc = jnp.where(vc, _nt(qh, kkc), NEG_INF)
        m = jnp.maximum(jnp.max(sp, axis=1, keepdims=True), jnp.max(sc, axis=1, keepdims=True))
        pp = jnp.exp2(sp - m)
        pc = jnp.exp2(sc - m)
        l = jnp.sum(pp, axis=1, keepdims=True) + jnp.sum(pc, axis=1, keepdims=True)
        outs.append((_mm(pp.astype(BF16), kvp.astype(BF16)) + _mm(pc.astype(BF16), kvc.astype(BF16))) / l)
    for g in range(2):
        o_ref[:, 128 * g:128 * (g + 1)] = _pair_out(outs[2 * g], outs[2 * g + 1])


def _win_prompt(cqr, win, t):
    L = cqr.shape[0]
    return pl.pallas_call(
        functools.partial(_win_prompt_kernel, t),
        grid=(L // t,),
        in_specs=[pl.BlockSpec((t, 256), lambda i: (i, 0)),
                  pl.BlockSpec((t, 128), lambda i: (jnp.maximum(i - 1, 0), 0)),
                  pl.BlockSpec((t, 128), lambda i: (i, 0))],
        out_specs=pl.BlockSpec((t, 256), lambda i: (i, 0)),
        out_shape=jax.ShapeDtypeStruct((L, 256), F32),
        compiler_params=_cp(("arbitrary",)),
        name="win_prompt",
    )(cqr, win, win)


def _cmp_kernel(tq, qpos0, topn, q_ref, kv_ref, o_ref, msk_ref):
    t = pl.program_id(1)
    bg = q_ref.shape[0]
    nb = kv_ref.shape[1]
    blk = _lane((tq, nb))
    qpos = qpos0 + t * tq + _row((tq, nb))
    valid = (blk * CMP_BLOCK + (CMP_BLOCK - 1)) <= qpos
    imps = []
    for b in range(bg):
        kv = kv_ref[b]
        kvb = kv.astype(BF16)
        kk = _kk(kv)
        q = q_ref[b]
        imp = jnp.zeros((tq, nb), F32)
        outs = []
        for h in range(C_HEADS):
            s = jnp.where(valid, _nt(_head_qb(q, h), kk), NEG_INF)
            m = jnp.max(s, axis=1, keepdims=True)
            p = jnp.where(valid, jnp.exp2(s - m), 0.0)
            p = p / jnp.maximum(jnp.sum(p, axis=1, keepdims=True), 1e-30)
            outs.append(_mm(p.astype(BF16), kvb))
            imp = imp + p
        for g in range(2):
            o_ref[b, :, 128 * g:128 * (g + 1)] = _pair_out(outs[2 * g], outs[2 * g + 1])
        imps.append(imp)
    rows = bg * tq
    imp = jnp.concatenate(imps, axis=0) if bg > 1 else imps[0]
    blk = _lane((rows, nb))
    cur = (qpos0 + t * tq + _row((rows, nb)) % tq) // CMP_BLOCK
    score = jnp.where(jnp.logical_or(blk == 0, blk == cur), FORCED_SCORE, jnp.where(blk < cur, imp, -1.0))
    chosen = jnp.zeros((rows, nb), F32)
    blkf = blk.astype(F32)
    for _ in range(topn):
        mx = jnp.max(score, axis=1, keepdims=True)
        idx = jnp.min(jnp.where(score == mx, blkf, float(nb)), axis=1, keepdims=True)
        hit = blkf == idx
        chosen = jnp.where(hit, 1.0, chosen)
        score = jnp.where(hit, -3.0e38, score)
    for b in range(bg):
        msk_ref[b] = chosen[b * tq:(b + 1) * tq]


def _cmp_attend(cq, kvcmp, qpos0, topn, tq, bg):
    B, L, _ = cq.shape
    nb = kvcmp.shape[1]
    return pl.pallas_call(
        functools.partial(_cmp_kernel, tq, qpos0, topn),
        grid=(B // bg, L // tq),
        in_specs=[pl.BlockSpec((bg, tq, 256), lambda b, t: (b, t, 0)),
                  pl.BlockSpec((bg, nb, 128), lambda b, t: (b, 0, 0))],
        out_specs=[pl.BlockSpec((bg, tq, 256), lambda b, t: (b, t, 0)),
                   pl.BlockSpec((bg, tq, nb), lambda b, t: (b, t, 0))],
        out_shape=[jax.ShapeDtypeStruct((B, L, 256), F32), jax.ShapeDtypeStruct((B, L, nb), F32)],
        compiler_params=_cp(("arbitrary", "arbitrary")),
        name="cmp_attend_select",
    )(cq, kvcmp)


def _cmp_weight(w_cmp):
    z = jnp.zeros((CMP_BLOCK, C_DH, C_DH), w_cmp.dtype)
    k_part = jnp.stack([jnp.concatenate([w_cmp[0], z], axis=-1), jnp.concatenate([z, w_cmp[1]], axis=-1)], axis=1)
    return k_part.reshape(CMP_BLOCK * 2 * C_DH, 2 * C_DH).astype(BF16)


def _summ_kernel(x_ref, w_ref, o_ref):
    o_ref[...] = _mm(x_ref[...].astype(BF16), w_ref[...])


def _summaries_prompt(cmp_rows, wc):
    L = cmp_rows.shape[0]
    nb = L // CMP_BLOCK
    kdim = CMP_BLOCK * 128
    tr = min(nb, 128)
    return pl.pallas_call(
        _summ_kernel,
        grid=(nb // tr,),
        in_specs=[pl.BlockSpec((tr, kdim), lambda i: (i, 0)), pl.BlockSpec((kdim, 128), lambda i: (0, 0))],
        out_specs=pl.BlockSpec((tr, 128), lambda i: (i, 0)),
        out_shape=jax.ShapeDtypeStruct((nb, 128), F32),
        compiler_params=_cp(("arbitrary",)),
        name="cmp_summaries_prompt",
    )(cmp_rows.reshape(nb, kdim), wc)


def _cmp_weight_by_feature(w_cmp, bpp):
    eye_b = jnp.eye(bpp, dtype=w_cmp.dtype)
    eye_c = jnp.eye(2, dtype=w_cmp.dtype)
    t = jnp.einsum("cjde,bB,cC->cdbjBCe", w_cmp, eye_b, eye_c)
    return t.reshape(2 * C_DH, bpp * CMP_BLOCK, bpp * 2 * C_DH).astype(BF16)


def _summ_paged_kernel(npg, layer, pt_ref, w_ref, pool_ref, o_ref, stage, sem):
    b = pl.program_id(0)
    slot = b % 2

    def page_copy(bb, p, sl):
        return pltpu.make_async_copy(pool_ref.at[layer, pt_ref[bb * npg + p]],
                                     stage.at[sl, :, pl.ds(p, 1), :], sem.at[sl])

    def start_all(bb, sl):
        lax.fori_loop(0, npg, lambda p, c: (page_copy(bb, p, sl).start(), c)[1], 0)

    def wait_all(bb, sl):
        lax.fori_loop(0, npg, lambda p, c: (page_copy(bb, p, sl).wait(), c)[1], 0)

    @pl.when(b == 0)
    def _():
        start_all(0, 0)

    @pl.when(b + 1 < pl.num_programs(0))
    def _():
        start_all(b + 1, 1 - slot)

    wait_all(b, slot)

    def body(cd, acc):
        return acc + _mm(stage[slot, cd].astype(BF16), w_ref[cd])

    o_ref[0] = lax.fori_loop(0, stage.shape[1], body, jnp.zeros(o_ref.shape[1:], F32), unroll=4)


def _summaries_paged(pool, layer, page_table, w_feat):
    depth, n_pool, page = pool.shape[:3]
    B, npg = page_table.shape
    nfeat = 2 * C_DH
    view = _feature_major(pool).reshape(depth, n_pool, nfeat, 1, page)
    wout = w_feat.shape[2]
    grid_spec = pltpu.PrefetchScalarGridSpec(
        num_scalar_prefetch=1, grid=(B,),
        in_specs=[pl.BlockSpec(w_feat.shape, lambda b, pt: (0, 0, 0)),
                  pl.BlockSpec(memory_space=pl.ANY)],
        out_specs=pl.BlockSpec((1, npg, wout), lambda b, pt: (b, 0, 0)),
        scratch_shapes=[pltpu.VMEM((2, nfeat, npg, page), F32), pltpu.SemaphoreType.DMA((2,))])
    out = pl.pallas_call(
        functools.partial(_summ_paged_kernel, npg, layer),
        grid_spec=grid_spec,
        out_shape=jax.ShapeDtypeStruct((B, npg, wout), F32),
        compiler_params=_cp(("arbitrary",), 56),
        name="cmp_summaries_paged",
    )(page_table.reshape(-1), w_feat, view)
    return out.reshape(B, npg * (page // CMP_BLOCK), 128)


def _hgrn_kernel(C, nchunk, q_ref, k_ref, v_ref, g_ref, s0_ref, o_ref, st_ref, st_sc):
    t = pl.program_id(1)

    @pl.when(t == 0)
    def _():
        st_sc[...] = s0_ref[0]

    W = B_HEADS * B_DK
    head_mask = (_row((W, W)) // B_DV) == (_lane((W, W)) // B_DK)
    ones_blk = jnp.where(head_mask, 1.0, 0.0).astype(BF16)
    tri = jnp.where(_row((C, C)) >= _lane((C, C)), 1.0, 0.0)
    srow = _row((C, W))

    def chunk(c, carry):
        r0 = pl.multiple_of(c * C, C)
        q = q_ref[0, pl.ds(r0, C), :]
        k = k_ref[0, pl.ds(r0, C), :]
        v = v_ref[0, pl.ds(r0, C), :]
        g = g_ref[0, pl.ds(r0, C), :]
        G = lax.dot_general(tri, g, (((1,), (0,)), ((), ())), preferred_element_type=F32,
                            precision=lax.Precision.HIGHEST)
        st = st_sc[...]
        ps = []
        for tt in range(C):
            d = jnp.where(srow <= tt, G[tt:tt + 1, :] - G, NEG_INF)
            ps.append(q[tt:tt + 1, :] * jnp.exp(d) * k)
        P = jnp.concatenate(ps, axis=0)
        R = _mm(P.astype(BF16), ones_blk)
        o_intra = jnp.sum(R.reshape(C, C, W) * v[None, :, :], axis=1)
        o_inter = _nt((q * jnp.exp(G)).astype(BF16), st.astype(BF16))
        o_ref[0, pl.ds(r0, C), :] = o_intra + o_inter
        g_last = G[C - 1:C, :]
        upd = _tn(v.astype(BF16), (k * jnp.exp(g_last - G)).astype(BF16))
        st_sc[...] = st * jnp.exp(g_last) + jnp.where(head_mask, upd, 0.0)
        return carry

    lax.fori_loop(0, nchunk, chunk, 0, unroll=min(nchunk, 8))

    @pl.when(t == pl.num_programs(1) - 1)
    def _():
        st_ref[0] = st_sc[...]


def _hgrn(hq, hk, hv, hf, s0t, C, tr):
    B, L, W = hq.shape
    seq = pl.BlockSpec((1, tr, W), lambda b, t: (b, t, 0))
    st = pl.BlockSpec((1, W, W), lambda b, t: (b, 0, 0))
    return pl.pallas_call(
        functools.partial(_hgrn_kernel, C, tr // C),
        grid=(B, L // tr),
        in_specs=[seq, seq, seq, seq, st],
        out_specs=[seq, st],
        out_shape=[jax.ShapeDtypeStruct((B, L, W), F32), jax.ShapeDtypeStruct((B, W, W), F32)],
        scratch_shapes=[pltpu.VMEM((W, W), F32)],
        compiler_params=_cp(("arbitrary", "arbitrary")),
        name="hgrn2",
    )(hq, hk, hv, hf, s0t)


def _state_to_blockdiag_t(s):
    B = s.shape[0]
    eye = jnp.eye(B_HEADS, dtype=s.dtype)
    return jnp.einsum("bhkv,hg->bhvgk", s, eye).reshape(B, B_HEADS * B_DV, B_HEADS * B_DK)


def _blockdiag_t_to_state(st):
    B = st.shape[0]
    st5 = st.reshape(B, B_HEADS, B_DV, B_HEADS, B_DK)
    return jnp.stack([st5[:, h, :, h, :] for h in range(B_HEADS)], axis=1).transpose(0, 1, 3, 2)


def _diff_dec_kernel(G, page, lam_init, pt_ref, q_ref, kn_ref, vn_ref, dl_ref, *rest):
    kps, vps = rest[:G], rest[G:2 * G]
    o_ref = rest[2 * G]
    m_sc, l_sc, acc_sc = rest[2 * G + 1:]
    s_idx = pl.program_id(1)
    nq = q_ref.shape[1]

    @pl.when(s_idx == 0)
    def _():
        m_sc[...] = jnp.full(m_sc.shape, NEG_INF, F32)
        l_sc[...] = jnp.zeros(l_sc.shape, F32)
        acc_sc[...] = jnp.zeros(acc_sc.shape, F32)

    qparts = []
    for h in range(A_HEADS):
        qp = q_ref[0, :, 128 * h:128 * (h + 1)]
        lane = _lane(qp.shape)
        qparts += [jnp.where(lane < 64, qp, 0.0), jnp.where(lane >= 64, qp, 0.0)]
    qall = jnp.concatenate(qparts, axis=0).astype(BF16)
    rows = qall.shape[0]
    rpp = page * A_HEADS
    same_head = (_lane((rows, G * rpp)) % A_HEADS) == (_row((rows, G * rpp)) // (2 * nq))
    kcat = jnp.concatenate([kps[g][...].astype(BF16) for g in range(G)], axis=0)
    vcat = jnp.concatenate([vps[g][...].astype(BF16) for g in range(G)], axis=0)
    s = jnp.where(same_head, _nt(qall, kcat), NEG_INF)
    p, alpha, m_next, l_next = _online(s, m_sc[:, :1], l_sc[:, :1])
    pv = _mm(p.astype(BF16), vcat)
    acc_sc[...] = acc_sc[...] * alpha + pv
    m_sc[...] = jnp.broadcast_to(m_next, m_sc.shape)
    l_sc[...] = jnp.broadcast_to(l_next, l_sc.shape)

    @pl.when(s_idx == pl.num_programs(1) - 1)
    def _():
        lam = _diff_lambda(dl_ref[...], lam_init)
        kn = jnp.concatenate([kn_ref[0, :, 128 * h:128 * (h + 1)] for h in range(A_HEADS)], axis=0)
        vn = jnp.concatenate([vn_ref[0, :, 128 * h:128 * (h + 1)] for h in range(A_HEADS)], axis=0)
        nk = A_HEADS * nq
        r, c = _row((rows, nk)), _lane((rows, nk))
        ok = jnp.logical_and(c // nq == r // (2 * nq), c % nq <= r % nq)
        sn = jnp.where(ok, _nt(qall, kn.astype(BF16)), NEG_INF)
        p2, alpha2, m2, l2 = _online(sn, m_sc[:, :1], l_sc[:, :1])
        o = (acc_sc[...] * alpha2 + _mm(p2.astype(BF16), vn.astype(BF16))) / l2
        for h in range(A_HEADS):
            o_ref[0, :, 128 * h:128 * (h + 1)] = (o[2 * nq * h:2 * nq * h + nq]
                                                  - lam * o[2 * nq * h + nq:2 * nq * (h + 1)])


def _diff_decode(aq, ak_new, av_new, dl, lam_init, pool_k, pool_v, layer, page_table, G):
    B, nq, _ = aq.shape
    depth, n_pool, page = pool_k.shape[:3]
    npg = page_table.shape[1]
    rpp = page * A_HEADS
    kview = pool_k.reshape(depth, n_pool, rpp, 128)
    vview = pool_v.reshape(depth, n_pool, rpp, 128)
    pt = page_table.reshape(-1)
    rows = A_HEADS * 2 * nq

    def page_spec(g):
        return pl.BlockSpec((None, None, rpp, 128), lambda b, s, pt: (layer, pt[b * npg + s * G + g], 0, 0))

    tok = pl.BlockSpec((1, nq, 512), lambda b, s, pt: (b, 0, 0))
    grid_spec = pltpu.PrefetchScalarGridSpec(
        num_scalar_prefetch=1, grid=(B, npg // G),
        in_specs=[tok, tok, tok, pl.BlockSpec(dl.shape, lambda b, s, pt: (0, 0))]
        + [page_spec(g) for g in range(G)] * 2,
        out_specs=tok,
        scratch_shapes=[pltpu.VMEM((rows, LANES), F32), pltpu.VMEM((rows, LANES), F32),
                        pltpu.VMEM((rows, 128), F32)])
    return pl.pallas_call(
        functools.partial(_diff_dec_kernel, G, page, lam_init),
        grid_spec=grid_spec,
        out_shape=jax.ShapeDtypeStruct((B, nq, 512), F32),
        compiler_params=_cp(("arbitrary", "arbitrary")),
        name="diff_decode",
    )(pt, aq, ak_new, av_new, dl, *([kview] * G), *([vview] * G))


def _sel_dec_kernel(G, page, pt_ref, q_ref, new_ref, msk_ref, *rest):
    pages = rest[:G]
    o_ref = rest[G]
    m_sc, l_sc, acc_sc = rest[G + 1:]
    s_idx = pl.program_id(1)
    nq = q_ref.shape[1]
    rows = C_HEADS * nq
    bpp = page // CMP_BLOCK

    @pl.when(s_idx == 0)
    def _():
        m_sc[...] = jnp.full(m_sc.shape, NEG_INF, F32)
        l_sc[...] = jnp.zeros(l_sc.shape, F32)
        acc_sc[...] = jnp.zeros(acc_sc.shape, F32)

    qall = _stack_low_q(q_ref[0])
    nb = msk_ref.shape[2]
    nkeys = G * page
    msk = msk_ref[0]
    msk4 = jnp.concatenate([msk] * C_HEADS, axis=0).astype(BF16)
    expand = jnp.where(_row((nb, nkeys)) == s_idx * (G * bpp) + _lane((nb, nkeys)) // CMP_BLOCK, 1.0, 0.0)
    valid = _mm(msk4, expand.astype(BF16)) > 0.5
    kvcat = jnp.concatenate([pages[g][...].astype(BF16) for g in range(G)], axis=1)
    s = jnp.where(valid, _mm(qall, kvcat), NEG_INF)
    p, alpha, m_next, l_next = _online(s, m_sc[:, :1], l_sc[:, :1])
    pv = _nt(p.astype(BF16), kvcat)
    acc_sc[...] = acc_sc[...] * alpha + pv
    m_sc[...] = jnp.broadcast_to(m_next, m_sc.shape)
    l_sc[...] = jnp.broadcast_to(l_next, l_sc.shape)

    @pl.when(s_idx == pl.num_programs(1) - 1)
    def _():
        kvn = new_ref[0].astype(BF16)
        causal = _lane((rows, nq)) <= (_row((rows, nq)) % nq)
        sn = jnp.where(causal, _nt(qall, kvn), NEG_INF)
        p2, alpha2, m2, l2 = _online(sn, m_sc[:, :1], l_sc[:, :1])
        o = (acc_sc[...] * alpha2 + _mm(p2.astype(BF16), kvn)) / l2
        for g in range(2):
            o_ref[0, :, 128 * g:128 * (g + 1)] = _pair_out(o[2 * g * nq:(2 * g + 1) * nq],
                                                           o[(2 * g + 1) * nq:(2 * g + 2) * nq])


def _sel_decode(cqr, sel_new, selmask, pool, layer, page_table, G):
    B, nq, _ = cqr.shape
    depth, n_pool, page = pool.shape[:3]
    npg = page_table.shape[1]
    nb = selmask.shape[2]
    view = _feature_major(pool)
    pt = page_table.reshape(-1)
    rows = C_HEADS * nq

    def page_spec(g):
        return pl.BlockSpec((None, None, 128, page), lambda b, s, pt: (layer, pt[b * npg + s * G + g], 0, 0))

    grid_spec = pltpu.PrefetchScalarGridSpec(
        num_scalar_prefetch=1, grid=(B, npg // G),
        in_specs=[pl.BlockSpec((1, nq, 256), lambda b, s, pt: (b, 0, 0)),
                  pl.BlockSpec((1, nq, 128), lambda b, s, pt: (b, 0, 0)),
                  pl.BlockSpec((1, nq, nb), lambda b, s, pt: (b, 0, 0))] + [page_spec(g) for g in range(G)],
        out_specs=pl.BlockSpec((1, nq, 256), lambda b, s, pt: (b, 0, 0)),
        scratch_shapes=[pltpu.VMEM((rows, LANES), F32), pltpu.VMEM((rows, LANES), F32),
                        pltpu.VMEM((rows, 128), F32)])
    return pl.pallas_call(
        functools.partial(_sel_dec_kernel, G, page),
        grid_spec=grid_spec,
        out_shape=jax.ShapeDtypeStruct((B, nq, 256), F32),
        compiler_params=_cp(("arbitrary", "arbitrary")),
        name="sel_decode",
    )(pt, cqr, sel_new, selmask, *([view] * G))


def _win_dec_kernel(q_ref, buf_ref, new_ref, o_ref):
    nq = q_ref.shape[1]
    W = buf_ref.shape[2]
    rows = C_HEADS * nq
    qall = _stack_low_q(q_ref[0])
    kvb = buf_ref[0].astype(BF16)
    kvn = new_ref[0].astype(BF16)
    tok = _row((rows, W)) % nq
    sb = jnp.where(_lane((rows, W)) > tok, _mm(qall, kvb), NEG_INF)
    sn = jnp.where(_lane((rows, nq)) <= _row((rows, nq)) % nq, _nt(qall, kvn), NEG_INF)
    m = jnp.maximum(jnp.max(sb, axis=1, keepdims=True), jnp.max(sn, axis=1, keepdims=True))
    pb, pn = jnp.exp2(sb - m), jnp.exp2(sn - m)
    l = jnp.sum(pb, axis=1, keepdims=True) + jnp.sum(pn, axis=1, keepdims=True)
    o = (_nt(pb.astype(BF16), kvb) + _mm(pn.astype(BF16), kvn)) / l
    for g in range(2):
        o_ref[0, :, 128 * g:128 * (g + 1)] = _pair_out(o[2 * g * nq:(2 * g + 1) * nq],
                                                       o[(2 * g + 1) * nq:(2 * g + 2) * nq])


def _win_decode(cqr, win_buf, win_new):
    B, nq, _ = cqr.shape
    W = win_buf.shape[2]
    return pl.pallas_call(
        _win_dec_kernel,
        grid=(B,),
        in_specs=[pl.BlockSpec((1, nq, 256), lambda b: (b, 0, 0)),
                  pl.BlockSpec((1, 128, W), lambda b: (b, 0, 0)),
                  pl.BlockSpec((1, nq, 128), lambda b: (b, 0, 0))],
        out_specs=pl.BlockSpec((1, nq, 256), lambda b: (b, 0, 0)),
        out_shape=jax.ShapeDtypeStruct((B, nq, 256), F32),
        compiler_params=_cp(("arbitrary",)),
        name="win_decode",
    )(cqr, win_buf, win_new)


def _merge_kernel(lam_init, x_ref, g1_ref, ao_ref, bo_ref, hg_ref, oc_ref, os_ref, ow_ref, cg_ref,
                  sub_ref, hn_ref, w_ref, o_ref):
    tm = x_ref.shape[0]
    sub = sub_ref[...]
    y = jnp.zeros((tm, w_ref.shape[1]), F32)
    for h in range(A_HEADS):
        a = ao_ref[:, 128 * h:128 * (h + 1)]
        a = a * lax.rsqrt(jnp.mean(a * a, axis=-1, keepdims=True) + RMS_EPS) * sub * (1.0 - lam_init)
        y = y + _mm(a.astype(BF16), w_ref[128 * h:128 * (h + 1), :])
    lane = _lane((tm, 128))
    lo = lane < 64
    hn = hn_ref[...]
    for g in range(2):
        b = bo_ref[:, 128 * g:128 * (g + 1)]
        b2 = b * b
        s_lo = jnp.sum(jnp.where(lo, b2, 0.0), axis=-1, keepdims=True)
        s_hi = jnp.sum(jnp.where(lo, 0.0, b2), axis=-1, keepdims=True)
        ms = jnp.where(lo, s_lo, s_hi) * (1.0 / B_DV)
        b = b * lax.rsqrt(ms + RMS_EPS) * hn * _silu(hg_ref[:, 128 * g:128 * (g + 1)])
        y = y + _mm(b.astype(BF16), w_ref[512 + 128 * g:512 + 128 * (g + 1), :])
    sig = jax.nn.sigmoid(cg_ref[...])
    for g in range(2):
        c = jnp.zeros((tm, 128), F32)
        for br, ref in enumerate((oc_ref, os_ref, ow_ref)):
            ge = jnp.broadcast_to(sig[:, 6 * g + br:6 * g + br + 1], (tm, 128))
            go = jnp.broadcast_to(sig[:, 6 * g + 3 + br:6 * g + 3 + br + 1], (tm, 128))
            c = c + jnp.where(lo, ge, go) * ref[:, 128 * g:128 * (g + 1)]
        y = y + _mm(c.astype(BF16), w_ref[768 + 128 * g:768 + 128 * (g + 1), :])
    o_ref[...] = x_ref[...] + g1_ref[...] * y


def _merge(lam_init, x, g1, a_o, b_o, hg, o_cmp, o_sel, o_win, cg, sub, hn2, w_bf, tm):
    rows, d = x.shape
    mod_spec = (pl.BlockSpec((1, d), lambda i: (0, 0)) if g1.shape[0] == 1
                else pl.BlockSpec((tm, d), lambda i: (i, 0)))

    def rs(w):
        return pl.BlockSpec((tm, w), lambda i: (i, 0))

    def full(a):
        return pl.BlockSpec(a.shape, lambda i: (0, 0))

    return pl.pallas_call(
        functools.partial(_merge_kernel, lam_init),
        grid=(rows // tm,),
        in_specs=[rs(d), mod_spec, rs(512), rs(256), rs(256), rs(256), rs(256), rs(256), rs(12),
                  full(sub), full(hn2), full(w_bf)],
        out_specs=rs(d),
        out_shape=jax.ShapeDtypeStruct((rows, d), F32),
        compiler_params=_cp(("arbitrary",)),
        name="merge_out_proj",
    )(x, g1, a_o, b_o, hg, o_cmp, o_sel, o_win, cg, sub, hn2, w_bf)


def _ffn_kernel(fc, x_ref, sc_ref, sh_ref, g2_ref, nw_ref, wgu_ref, wd_ref, o_ref):
    x = x_ref[...]
    hb = _rms_mod(x, nw_ref[...], sc_ref[...], sh_ref[...]).astype(BF16)
    dff = wd_ref.shape[0]
    y = jnp.zeros(x.shape, F32)
    for c in range(dff // fc):
        g = _mm(hb, wgu_ref[:, c * fc:(c + 1) * fc])
        u = _mm(hb, wgu_ref[:, dff + c * fc:dff + (c + 1) * fc])
        y = y + _mm((_silu(g) * u).astype(BF16), wd_ref[c * fc:(c + 1) * fc, :])
    o_ref[...] = x + g2_ref[...] * y


def _ffn_dense(x, sc, sh, g2, nw, wgu_bf, wd_bf, tm, fc):
    rows, d = x.shape
    mod_spec = (pl.BlockSpec((1, d), lambda i: (0, 0)) if sc.shape[0] == 1
                else pl.BlockSpec((tm, d), lambda i: (i, 0)))
    return pl.pallas_call(
        functools.partial(_ffn_kernel, fc),
        grid=(rows // tm,),
        in_specs=[pl.BlockSpec((tm, d), lambda i: (i, 0)), mod_spec, mod_spec, mod_spec,
                  pl.BlockSpec((1, d), lambda i: (0, 0)),
                  pl.BlockSpec(wgu_bf.shape, lambda i: (0, 0)),
                  pl.BlockSpec(wd_bf.shape, lambda i: (0, 0))],
        out_specs=pl.BlockSpec((tm, d), lambda i: (i, 0)),
        out_shape=jax.ShapeDtypeStruct((rows, d), F32),
        compiler_params=_cp(("arbitrary",), 56),
        name="ffn_dense",
    )(x, sc, sh, g2, nw, wgu_bf, wd_bf)


def _router_kernel(x_ref, sc_ref, sh_ref, nw_ref, wr_ref, br_ref, gate_ref, pos_ref, cnt_ref):
    h = _rms_mod(x_ref[...], nw_ref[...], sc_ref[...], sh_ref[...])
    tm = h.shape[0]
    logits = lax.dot_general(wr_ref[...], h, (((1,), (1,)), ((), ())), preferred_element_type=F32,
                             precision=lax.Precision.HIGHEST) + br_ref[...]
    ne = logits.shape[0]
    row = _row((ne, tm)).astype(F32)
    m1 = jnp.max(logits, axis=0, keepdims=True)
    i1 = jnp.min(jnp.where(logits == m1, row, float(ne)), axis=0, keepdims=True)
    hit1 = row == i1
    l2 = jnp.where(hit1, -3.0e38, logits)
    m2 = jnp.max(l2, axis=0, keepdims=True)
    i2 = jnp.min(jnp.where(l2 == m2, row, float(ne)), axis=0, keepdims=True)
    hit2 = row == i2
    e2 = jnp.exp(m2 - m1)
    den = 1.0 + e2
    gate_ref[...] = jnp.where(hit1, 1.0 / den, jnp.where(hit2, e2 / den, 0.0))
    member = jnp.logical_or(hit1, hit2)
    mem = jnp.where(member, 1.0, 0.0)
    before = jnp.where(_row((tm, tm)) < _lane((tm, tm)), 1.0, 0.0).astype(BF16)
    pos = _mm(mem.astype(BF16), before)
    pos_ref[...] = jnp.where(member, pos, -1.0)
    cnt_ref[0] = jnp.broadcast_to(jnp.sum(mem, axis=1, keepdims=True), (ne, LANES)).astype(jnp.int32)


def _router(x, sc, sh, nw, w_router, b_router, tm):
    rows, d = x.shape
    ne = w_router.shape[1]
    nt = rows // tm
    mod_spec = (pl.BlockSpec((1, d), lambda i: (0, 0)) if sc.shape[0] == 1
                else pl.BlockSpec((tm, d), lambda i: (i, 0)))
    return pl.pallas_call(
        _router_kernel,
        grid=(nt,),
        in_specs=[pl.BlockSpec((tm, d), lambda i: (i, 0)), mod_spec, mod_spec,
                  pl.BlockSpec((1, d), lambda i: (0, 0)),
                  pl.BlockSpec((ne, d), lambda i: (0, 0)),
                  pl.BlockSpec((ne, 1), lambda i: (0, 0))],
        out_specs=[pl.BlockSpec((ne, tm), lambda i: (0, i)), pl.BlockSpec((ne, tm), lambda i: (0, i)),
                   pl.BlockSpec((1, ne, LANES), lambda i: (i, 0, 0))],
        out_shape=[jax.ShapeDtypeStruct((ne, rows), F32), jax.ShapeDtypeStruct((ne, rows), F32),
                   jax.ShapeDtypeStruct((nt, ne, LANES), jnp.int32)],
        compiler_params=_cp(("arbitrary",)),
        name="moe_router",
    )(x, sc, sh, nw, w_router.T, b_router.reshape(ne, 1))


def _moe_kernel(nf, final_norm, cnt_ref, x_ref, sc_ref, sh_ref, g2_ref, nw_ref, nfin_ref, gate_ref, pos_ref,
                wg_ref, wu_ref, wd_ref, o_ref, hb_sc, acc_sc, xs_sc, ys_sc):
    i, e, f = pl.program_id(0), pl.program_id(1), pl.program_id(2)
    ne = pl.num_programs(1)
    tm = x_ref.shape[0]
    nsub = (cnt_ref[i * ne + e] + (MOE_SUB - 1)) // MOE_SUB

    @pl.when(jnp.logical_and(e == 0, f == 0))
    def _():
        hb_sc[...] = _rms_mod(x_ref[...], nw_ref[...], sc_ref[...], sh_ref[...]).astype(BF16)
        acc_sc[...] = jnp.zeros(acc_sc.shape, F32)

    def process(r0, nrows):
        rows = pl.ds(r0, nrows)

        def onehot():
            pos_row = pos_ref[pl.ds(e, 1), :]
            return pos_row == (_row((nrows, tm)) + r0).astype(F32)

        @pl.when(f == 0)
        def _():
            sel = jnp.where(onehot(), 1.0, 0.0).astype(BF16)
            xs_sc[rows, :] = _mm(sel, hb_sc[...]).astype(BF16)
            ys_sc[rows, :] = jnp.zeros((nrows, ys_sc.shape[1]), F32)

        xs = xs_sc[rows, :]
        act = (_silu(_mm(xs, wg_ref[0])) * _mm(xs, wu_ref[0])).astype(BF16)
        ys_sc[rows, :] += _mm(act, wd_ref[0])

        @pl.when(f == nf - 1)
        def _():
            oh = onehot()
            gate_row = gate_ref[pl.ds(e, 1), :]
            gr = jnp.sum(jnp.where(oh, gate_row, 0.0), axis=1, keepdims=True)
            y = ys_sc[rows, :] * gr
            y_hi = y.astype(BF16)
            y_lo = (y - y_hi.astype(F32)).astype(BF16)
            sel = jnp.where(oh, 1.0, 0.0).astype(BF16)
            acc_sc[...] += _tn(sel, y_hi) + _tn(sel, y_lo)

    cap = min(MOE_CAP, tm // MOE_SUB)
    pl.when(nsub > 0)(lambda: process(0, cap * MOE_SUB))

    def body(j, carry):
        process(pl.multiple_of(j * MOE_SUB, MOE_SUB), MOE_SUB)
        return carry

    lax.fori_loop(cap, jnp.maximum(nsub, cap), body, 0)

    @pl.when(jnp.logical_and(e == ne - 1, f == nf - 1))
    def _():
        xo = x_ref[...] + g2_ref[...] * acc_sc[...]
        if final_norm:
            xo = xo * lax.rsqrt(jnp.mean(xo * xo, axis=-1, keepdims=True) + RMS_EPS) * nfin_ref[...]
        o_ref[...] = xo


def _moe(x, sc, sh, g2, nw, nfin, gate_t, pos_t, counts, wgu_bf, wd_bf, tm, nf, final_norm):
    rows, d = x.shape
    ne, _, dff2 = wgu_bf.shape
    dff = dff2 // 2
    fc = dff // nf
    nt = rows // tm
    mod_spec = (pl.BlockSpec((1, d), lambda i, e, f, c: (0, 0)) if sc.shape[0] == 1
                else pl.BlockSpec((tm, d), lambda i, e, f, c: (i, 0)))
    vec = pl.BlockSpec((1, d), lambda i, e, f, c: (0, 0))
    route = pl.BlockSpec((ne, tm), lambda i, e, f, c: (0, i))
    grid_spec = pltpu.PrefetchScalarGridSpec(
        num_scalar_prefetch=1, grid=(nt, ne, nf),
        in_specs=[pl.BlockSpec((tm, d), lambda i, e, f, c: (i, 0)), mod_spec, mod_spec, mod_spec, vec, vec,
                  route, route,
                  pl.BlockSpec((1, d, fc), lambda i, e, f, c: (e, 0, f)),
                  pl.BlockSpec((1, d, fc), lambda i, e, f, c: (e, 0, nf + f)),
                  pl.BlockSpec((1, fc, d), lambda i, e, f, c: (e, f, 0))],
        out_specs=pl.BlockSpec((tm, d), lambda i, e, f, c: (i, 0)),
        scratch_shapes=[pltpu.VMEM((tm, d), BF16), pltpu.VMEM((tm, d), F32),
                        pltpu.VMEM((tm, d), BF16), pltpu.VMEM((tm, d), F32)])
    return pl.pallas_call(
        functools.partial(_moe_kernel, nf, final_norm),
        grid_spec=grid_spec,
        out_shape=jax.ShapeDtypeStruct((rows, d), F32),
        compiler_params=_cp(("arbitrary", "arbitrary", "arbitrary"), 56),
        name="moe_experts",
    )(counts, x, sc, sh, g2, nw, nfin, gate_t, pos_t, wgu_bf, wgu_bf, wd_bf)


def _norm_kernel(x_ref, w_ref, o_ref):
    x = x_ref[...]
    o_ref[...] = x * lax.rsqrt(jnp.mean(x * x, axis=-1, keepdims=True) + RMS_EPS) * w_ref[...]


def _final_norm(x, w, tm):
    rows, d = x.shape
    return pl.pallas_call(
        _norm_kernel, grid=(rows // tm,),
        in_specs=[pl.BlockSpec((tm, d), lambda i: (i, 0)), pl.BlockSpec((1, d), lambda i: (0, 0))],
        out_specs=pl.BlockSpec((tm, d), lambda i: (i, 0)),
        out_shape=jax.ShapeDtypeStruct((rows, d), F32),
        compiler_params=_cp(("arbitrary",)),
        name="final_norm",
    )(x, w)


def _rope_tables(pos):
    inv = ROPE_THETA ** (-jnp.arange(0, 64, 2, dtype=F32) / 64)
    ang = pos.astype(F32)[:, None] * inv[None, :]
    cos, sin = jnp.cos(ang), jnp.sin(ang)
    return jnp.tile(cos, (1, 4)), jnp.tile(jnp.concatenate([-sin, sin], axis=1), (1, 2))


def _pick(n, pref):
    t = min(n, pref)
    while n % t:
        t //= 2
    return t


def kernel(x_prompt, x_sample, cache_diff_k, cache_diff_v, cache_nsa_cmp, cache_nsa_sel, state_nsa_win, state_hgrn, page_table, c_prompt, c_sample, w_in, w_out, w_cmp, diff_lambda, diff_subln, hgrn_lb_logits, hgrn_norm, norm_mix, norm_ffn, norm_final, w_ada, b_ada, ffn_w_gate_up, ffn_w_down, moe_w_router, moe_b_router, moe_w_gate_up, moe_w_down):
    depth = w_in.shape[0]
    _, L, D = x_prompt.shape
    Bs, Ls, _ = x_sample.shape
    page = cache_diff_k.shape[2]
    past = page_table.shape[1] * page
    rows_s = Bs * Ls

    kvc_sample = [_summaries_paged(cache_nsa_cmp, l, page_table,
                                   _cmp_weight_by_feature(w_cmp[l], page // CMP_BLOCK)) for l in range(depth)]

    n_c = 1 + Bs
    n_cp = -(-n_c // 8) * 8
    c_all = jnp.concatenate([c_prompt, c_sample, jnp.zeros((n_cp - n_c, D), F32)], axis=0)
    mod = _ada(c_all, w_ada, b_ada)

    cos_p, sin_p = _rope_tables(jnp.arange(L))
    cos_s, sin_s = _rope_tables(past + (jnp.arange(rows_s) % Ls))

    xp = x_prompt.reshape(L, D)
    xs = x_sample.reshape(rows_s, D)
    tmp = _pick(L, 512)
    tq_a, tk_a = _pick(L, 512), _pick(L, 1024)
    tq_s, tk_s = _pick(L, 512), _pick(L, 1024)
    G_diff = _pick(page_table.shape[1], 16)
    G_sel = _pick(page_table.shape[1], 32)
    outs_p = [[] for _ in range(6)]
    outs_s = [[] for _ in range(6)]

    for l in range(depth):
        lam_init = 0.8 - 0.6 * math.exp(-0.3 * l)
        w_in_bf = w_in[l].astype(BF16)
        w_out_bf = w_out[l].astype(BF16)
        wc = _cmp_weight(w_cmp[l])
        dl = diff_lambda[l]
        sub = diff_subln[l].reshape(1, A_DV)
        hn2 = jnp.tile(hgrn_norm[l].reshape(1, B_DV), (1, 2))
        nmix = norm_mix[l].reshape(1, D)
        nffn = norm_ffn[l].reshape(1, D)
        mp = [m for m in jnp.split(mod[l, 0:1], 6, axis=-1)]
        ms = [jnp.repeat(m, Ls, axis=0) for m in jnp.split(mod[l, 1:1 + Bs], 6, axis=-1)]

        (aq, ak, av, hq, hk, hf, hv, hg, cq, cqr, cmpr, sel, win, cg) = _proj(
            l, xp, mp[1], mp[0], nmix, w_in_bf, cos_p, sin_p, hgrn_lb_logits, tmp)
        a_o = _diff_flash(aq, ak, av, dl, lam_init, tq_a, tk_a, A_HEADS)
        s0 = jnp.zeros((1, B_HEADS * B_DV, B_HEADS * B_DK), F32)
        b_o, st = _hgrn(hq[None], hk[None], hv[None], hf[None], s0, 16, _pick(L, 512))
        kvc = _summaries_prompt(cmpr, wc)
        o_cmp, selmask = _cmp_attend(cq[None], kvc[None], 0, TOP_N, _pick(L, 1024), 1)
        o_sel = _sel_flash(cqr, sel, selmask[0], tq_s, tk_s)
        o_win = _win_prompt(cqr, win, _pick(L, WINDOW))
        xp = _merge(lam_init, xp, mp[2], a_o, b_o[0], hg, o_cmp[0], o_sel, o_win, cg, sub, hn2, w_out_bf, tmp)
        W = min(WINDOW, L)
        for i, a in enumerate((ak.reshape(1, L, A_HEADS, 2 * A_DK), av.reshape(1, L, A_HEADS, A_DV),
                               cmpr.reshape(1, L, 2, C_DH), sel.reshape(1, L, 2, C_DH),
                               win[L - W:].reshape(1, W, 2, C_DH), _blockdiag_t_to_state(st))):
            outs_p[i].append(a)

        (aq, ak, av, hq, hk, hf, hv, hg, cq, cqr, cmpr, sel, win, cg) = _proj(
            l, xs, ms[1], ms[0], nmix, w_in_bf, cos_s, sin_s, hgrn_lb_logits, rows_s)
        b3 = lambda a: a.reshape(Bs, Ls, a.shape[-1])
        a_o = _diff_decode(b3(aq).astype(F32), b3(ak), b3(av), dl, lam_init, cache_diff_k, cache_diff_v, l,
                           page_table, G_diff)
        b_o, st = _hgrn(b3(hq), b3(hk), b3(hv), b3(hf), _state_to_blockdiag_t(state_hgrn[l]), Ls, Ls)
        o_cmp, selmask = _cmp_attend(b3(cq).astype(F32), kvc_sample[l], past, TOP_N - 1, Ls, _pick(Bs, 16))
        o_sel = _sel_decode(b3(cqr).astype(F32), b3(sel), selmask, cache_nsa_sel, l, page_table, G_sel)
        o_win = _win_decode(b3(cqr).astype(F32), _feature_major(state_nsa_win[l]), b3(win))
        xs = _merge(lam_init, xs, ms[2], a_o.reshape(rows_s, 512), b_o.reshape(rows_s, 256), hg,
                    o_cmp.reshape(rows_s, 256), o_sel.reshape(rows_s, 256), o_win.reshape(rows_s, 256), cg,
                    sub, hn2, w_out_bf, rows_s)
        win_all = jnp.concatenate([state_nsa_win[l], win.reshape(Bs, Ls, 2, C_DH)], axis=1)[:, Ls:]
        for i, a in enumerate((ak.reshape(Bs, Ls, A_HEADS, 2 * A_DK), av.reshape(Bs, Ls, A_HEADS, A_DV),
                               cmpr.reshape(Bs, Ls, 2, C_DH), sel.reshape(Bs, Ls, 2, C_DH),
                               win_all, _blockdiag_t_to_state(st))):
            outs_s[i].append(a)

        j = l // 2
        last = l == depth - 1
        nfin = norm_final.reshape(1, D)
        if l % 2 == 0:
            wgu = ffn_w_gate_up[j].astype(BF16)
            wd = ffn_w_down[j].astype(BF16)
            fc = wd.shape[0] // 2
            xp = _ffn_dense(xp, mp[4], mp[3], mp[5], nffn, wgu, wd, tmp, fc)
            xs = _ffn_dense(xs, ms[4], ms[3], ms[5], nffn, wgu, wd, rows_s, fc)
            if last:
                xp = _final_norm(xp, nfin, tmp)
                xs = _final_norm(xs, nfin, rows_s)
        else:
            wgu = moe_w_gate_up[j].astype(BF16)
            wd = moe_w_down[j].astype(BF16)
            for which in range(2):
                x, m6, tm = (xp, mp, _pick(L, 1024)) if which == 0 else (xs, ms, rows_s)
                gate_t, pos_t, cnt = _router(x, m6[4], m6[3], nffn, moe_w_router[j], moe_b_router[j], tm)
                y = _moe(x, m6[4], m6[3], m6[5], nffn, nfin, gate_t, pos_t, cnt[:, :, 0].reshape(-1),
                         wgu, wd, tm, 2, last)
                if which == 0:
                    xp = y
                else:
                    xs = y
            if not last:
                pass

    y_prompt = xp.reshape(1, L, D)
    y_sample = xs.reshape(Bs, Ls, D)
    p_st = [jnp.stack(a, axis=0) for a in outs_p]
    s_st = [jnp.stack(a, axis=0) for a in outs_s]
    return (y_prompt, y_sample, *p_st, *s_st)
```

```python
import functools
import math

import numpy as np
import jax
import jax.numpy as jnp
from jax import lax
from jax.experimental import pallas as pl
from jax.experimental.pallas import tpu as pltpu

F32 = jnp.float32
BF16 = jnp.bfloat16

A_HEADS = 4
A_DK = 64
A_DV = 128
B_HEADS = 4
B_DK = 64
B_DV = 64
C_HEADS = 4
C_DH = 64
CMP_BLOCK = 64
TOP_N = 16
WINDOW = 512
ROPE_THETA = 10000.0
N_EXPERTS = 8
RMS_EPS = 1e-6
NEG_INF = -1e30
FORCED_SCORE = 1e4
F_FLOOR = 1e-20
QK_SCALE = 0.125 * math.log2(math.e)
LANES = 128
MOE_SUB = 128
MOE_CAP = 2

_SPL = (512, 512, 512, 256, 256, 256, 256, 256, 128, 128, 128, 12)
_OFF = tuple(int(v) for v in np.cumsum((0,) + _SPL))
IN_COLS = _OFF[-1]


def _cp(sem, vmem_mb=48):
    return pltpu.CompilerParams(dimension_semantics=sem, vmem_limit_bytes=vmem_mb * 1024 * 1024)


def _silu(x):
    return x * jax.nn.sigmoid(x)


def _nt(a, b):
    return lax.dot_general(a, b, (((1,), (1,)), ((), ())), preferred_element_type=F32)


def _tn(a, b):
    return lax.dot_general(a, b, (((0,), (0,)), ((), ())), preferred_element_type=F32)


def _mm(a, b):
    return jnp.dot(a, b, preferred_element_type=F32)


def _lane(shape):
    return lax.broadcasted_iota(jnp.int32, shape, 1)


def _row(shape):
    return lax.broadcasted_iota(jnp.int32, shape, 0)


def _rms_mod(x, nw, sc, sh):
    h = x * lax.rsqrt(jnp.mean(x * x, axis=-1, keepdims=True) + RMS_EPS) * nw
    return h * (1.0 + sc) + sh


def _ada_kernel(c_ref, w_ref, b_ref, o_ref):
    s = _silu(c_ref[...]).astype(BF16)
    o_ref[0] = _mm(s, w_ref[0].astype(BF16)) + b_ref[0]


def _ada(c_all, w_ada, b_ada):
    depth, d, n = w_ada.shape
    r = c_all.shape[0]
    tn = 1536
    return pl.pallas_call(
        _ada_kernel,
        grid=(depth, n // tn),
        in_specs=[pl.BlockSpec((r, d), lambda l, j: (0, 0)),
                  pl.BlockSpec((1, d, tn), lambda l, j: (l, 0, j)),
                  pl.BlockSpec((1, 1, tn), lambda l, j: (l, 0, j))],
        out_specs=pl.BlockSpec((1, r, tn), lambda l, j: (l, 0, j)),
        out_shape=jax.ShapeDtypeStruct((depth, r, n), F32),
        compiler_params=_cp(("arbitrary", "arbitrary")),
        name="ada_mod",
    )(c_all, w_ada, b_ada.reshape(depth, 1, n))


def _proj_kernel(layer, x_ref, sc_ref, sh_ref, nw_ref, w_ref, cos_ref, sin_ref, lbl_ref,
                 aq_ref, ak_ref, av_ref, hq_ref, hk_ref, hf_ref, hv_ref, hg_ref,
                 cq_ref, cqr_ref, cmp_ref, sel_ref, win_ref, cg_ref):
    hb = _rms_mod(x_ref[...], nw_ref[...], sc_ref[...], sh_ref[...]).astype(BF16)
    cos = cos_ref[...]
    sin = sin_ref[...]
    lane = _lane(cos.shape)
    first = (lane % 64) < 32

    def mm(i):
        return _mm(hb, w_ref[:, _OFF[i]:_OFF[i + 1]])

    def rope(y, c, s):
        rot = jnp.where(first, pltpu.roll(y, 96, 1), pltpu.roll(y, 32, 1))
        return y * c + rot * s

    y = mm(0)
    for g in range(4):
        aq_ref[:, 128 * g:128 * (g + 1)] = (rope(y[:, 128 * g:128 * (g + 1)], cos, sin) * QK_SCALE).astype(BF16)
    y = mm(1)
    for g in range(4):
        ak_ref[:, 128 * g:128 * (g + 1)] = rope(y[:, 128 * g:128 * (g + 1)], cos, sin)
    av_ref[...] = mm(2)
    hq_ref[...] = _silu(mm(3))
    lbl = lbl_ref[...]
    e = jnp.exp(lbl - jnp.max(lbl, axis=0, keepdims=True))
    sm = e / jnp.sum(e, axis=0, keepdims=True)
    lb = jnp.sum(sm[0:layer + 1], axis=0, keepdims=True) - sm[0:1]
    f = lb + (1.0 - lb) * jax.nn.sigmoid(mm(4))
    hf_ref[...] = jnp.log(jnp.maximum(f, F_FLOOR))
    hk_ref[...] = 1.0 - f
    hv_ref[...] = mm(5)
    hg_ref[...] = mm(6)
    y = mm(7)
    cq_ref[...] = (y * QK_SCALE).astype(BF16)
    for g in range(2):
        cqr_ref[:, 128 * g:128 * (g + 1)] = (rope(y[:, 128 * g:128 * (g + 1)], cos, sin) * QK_SCALE).astype(BF16)
    cmp_ref[...] = mm(8)
    cos_k = jnp.where(lane < 64, cos, 1.0)
    sin_k = jnp.where(lane < 64, sin, 0.0)
    sel_ref[...] = rope(mm(9), cos_k, sin_k)
    win_ref[...] = rope(mm(10), cos_k, sin_k)
    cg_ref[...] = mm(11)


def _proj(layer, x, sc, sh, nw, w_bf, cos, sin, lbl, tm):
    rows, d = x.shape
    mrows = sc.shape[0]
    mod_spec = (pl.BlockSpec((1, d), lambda i: (0, 0)) if mrows == 1
                else pl.BlockSpec((tm, d), lambda i: (i, 0)))
    widths = (512, 512, 512, 256, 256, 256, 256, 256, 256, 256, 128, 128, 128, 12)
    dtypes = (BF16, F32, F32, F32, F32, F32, F32, F32, BF16, BF16, F32, F32, F32, F32)
    return pl.pallas_call(
        functools.partial(_proj_kernel, layer),
        grid=(rows // tm,),
        in_specs=[pl.BlockSpec((tm, d), lambda i: (i, 0)), mod_spec, mod_spec,
                  pl.BlockSpec((1, d), lambda i: (0, 0)),
                  pl.BlockSpec((d, IN_COLS), lambda i: (0, 0)),
                  pl.BlockSpec((tm, LANES), lambda i: (i, 0)),
                  pl.BlockSpec((tm, LANES), lambda i: (i, 0)),
                  pl.BlockSpec(lbl.shape, lambda i: (0, 0))],
        out_specs=[pl.BlockSpec((tm, w), lambda i: (i, 0)) for w in widths],
        out_shape=[jax.ShapeDtypeStruct((rows, w), dt) for w, dt in zip(widths, dtypes)],
        compiler_params=_cp(("arbitrary",), 56),
        name="in_proj",
    )(x, sc, sh, nw, w_bf, cos, sin, lbl)


def _online(s, m_prev, l_prev):
    m_next = jnp.maximum(m_prev, jnp.max(s, axis=1, keepdims=True))
    p = jnp.exp2(s - m_next)
    alpha = jnp.exp2(m_prev - m_next)
    l_next = alpha * l_prev + jnp.sum(p, axis=1, keepdims=True)
    return p, alpha, m_next, l_next


def _flash_step(s, valid, m_prev, rhs_aug):
    n = s.shape[1]
    chunks = [s[:, LANES * c:LANES * (c + 1)] for c in range(n // LANES)]
    if valid is not None:
        chunks = [jnp.where(valid[:, LANES * c:LANES * (c + 1)], ch, NEG_INF) for c, ch in enumerate(chunks)]
    mx = chunks[0]
    for ch in chunks[1:]:
        mx = jnp.maximum(mx, ch)
    m_next = jnp.maximum(m_prev, jnp.max(mx, axis=1, keepdims=True))
    p = jnp.concatenate([jnp.exp2(ch - m_next).astype(BF16) for ch in chunks], axis=1)
    return _mm(p, rhs_aug), m_next, jnp.exp2(m_prev - m_next)


def _diff_lambda(dl, lam_init):
    a = jnp.sum(dl[0:1] * dl[1:2], axis=1, keepdims=True)
    b = jnp.sum(dl[2:3] * dl[3:4], axis=1, keepdims=True)
    return jnp.exp(a) - jnp.exp(b) + lam_init


def _causal_pairs(nq, tq, tk):
    ii, jj = [], []
    for i in range(nq):
        for j in range(((i + 1) * tq - 1) // tk + 1):
            ii.append(i)
            jj.append(j)
    return jnp.asarray(ii, jnp.int32), jnp.asarray(jj, jnp.int32)


def _causal_dispatch(step, i, j, tq, tk):
    needs_mask = (j * tk + tk - 1) > (i * tq)
    pl.when(jnp.logical_not(needs_mask))(lambda: step(False))
    if 2 * tq <= tk:
        narrow = (i * tq + tq - j * tk) <= tk // 2
        pl.when(jnp.logical_and(needs_mask, narrow))(lambda: step(True, tk // 2))
        pl.when(jnp.logical_and(needs_mask, jnp.logical_not(narrow)))(lambda: step(True))
    else:
        pl.when(needs_mask)(lambda: step(True))


def _diff_flash_kernel(tq, tk, lam_init, ii_ref, jj_ref, q_ref, k_ref, v_ref, dl_ref, o_ref,
                       m_sc, l_sc, acc_sc):
    s_idx = pl.program_id(1)
    i = ii_ref[s_idx]
    j = jj_ref[s_idx]
    hpg = q_ref.shape[1] // 128

    @pl.when(j == 0)
    def _():
        m_sc[...] = jnp.full(m_sc.shape, NEG_INF, F32)
        l_sc[...] = jnp.zeros(l_sc.shape, F32)
        acc_sc[...] = jnp.zeros(acc_sc.shape, F32)

    ones = jnp.ones((tk, LANES), BF16)
    low = _lane((tq, 128)) < 64

    def step(masked, nc=tk):
        valid = (j * tk + _lane((tq, nc))) <= (i * tq + _row((tq, nc))) if masked else None
        for h in range(hpg):
            q = q_ref[:, 128 * h:128 * (h + 1)]
            kb = k_ref[0:nc, 128 * h:128 * (h + 1)].astype(BF16)
            v_aug = jnp.concatenate([v_ref[0:nc, 128 * h:128 * (h + 1)].astype(BF16), ones[0:nc]], axis=1)
            for mi in range(2):
                c = 2 * h + mi
                qm = jnp.where(low if mi == 0 else jnp.logical_not(low), q, jnp.zeros_like(q))
                pv, m_next, alpha = _flash_step(_nt(qm, kb), valid, m_sc[c], v_aug)
                acc_sc[c] = acc_sc[c] * alpha + pv[:, :128]
                l_sc[c] = l_sc[c] * alpha + pv[:, 128:]
                m_sc[c] = m_next

    _causal_dispatch(step, i, j, tq, tk)

    @pl.when(j == ((i + 1) * tq - 1) // tk)
    def _():
        lam = _diff_lambda(dl_ref[...], lam_init)
        for h in range(hpg):
            o_ref[:, 128 * h:128 * (h + 1)] = (acc_sc[2 * h] / l_sc[2 * h]
                                               - lam * (acc_sc[2 * h + 1] / l_sc[2 * h + 1]))


def _diff_flash(aq, ak, av, dl, lam_init, tq, tk, hpg):
    L = aq.shape[0]
    nq = L // tq
    ii, jj = _causal_pairs(nq, tq, tk)
    w = 128 * hpg
    grid_spec = pltpu.PrefetchScalarGridSpec(
        num_scalar_prefetch=2,
        grid=(A_HEADS // hpg, int(ii.shape[0])),
        in_specs=[pl.BlockSpec((tq, w), lambda h, s, ii, jj: (ii[s], h)),
                  pl.BlockSpec((tk, w), lambda h, s, ii, jj: (jj[s], h)),
                  pl.BlockSpec((tk, w), lambda h, s, ii, jj: (jj[s], h)),
                  pl.BlockSpec(dl.shape, lambda h, s, ii, jj: (0, 0))],
        out_specs=pl.BlockSpec((tq, w), lambda h, s, ii, jj: (ii[s], h)),
        scratch_shapes=[pltpu.VMEM((2 * hpg, tq, LANES), F32), pltpu.VMEM((2 * hpg, tq, LANES), F32),
                        pltpu.VMEM((2 * hpg, tq, 128), F32)])
    return pl.pallas_call(
        functools.partial(_diff_flash_kernel, tq, tk, lam_init),
        grid_spec=grid_spec,
        out_shape=jax.ShapeDtypeStruct((L, 512), F32),
        compiler_params=_cp(("arbitrary", "arbitrary")),
        name="diff_flash",
    )(ii, jj, aq, ak, av, dl)


def _kk(kv):
    return jnp.where(_lane(kv.shape) < 64, kv, pltpu.roll(kv, 64, 1)).astype(BF16)


def _head_q(q_ref_or_val, h):
    qp = q_ref_or_val[:, 128 * (h // 2):128 * (h // 2 + 1)]
    lane = _lane(qp.shape)
    keep = (lane < 64) if h % 2 == 0 else (lane >= 64)
    return jnp.where(keep, qp, jnp.zeros_like(qp))


def _head_qb(q_ref_or_val, h):
    return _head_q(q_ref_or_val, h).astype(BF16)


def _pair_out(o_even, o_odd):
    return jnp.where(_lane(o_even.shape) < 64, pltpu.roll(o_even, 64, 1), o_odd)


def _stack_low_q(q):
    parts = []
    for g in range(C_HEADS // 2):
        qp = q[:, 128 * g:128 * (g + 1)]
        low = _lane(qp.shape) < 64
        parts += [jnp.where(low, qp, 0.0), jnp.where(low, pltpu.roll(qp, 64, 1), 0.0)]
    return jnp.concatenate(parts, axis=0).astype(BF16)


def _feature_major(a):
    n = a.ndim
    t = jnp.transpose(a, tuple(range(n - 3)) + (n - 2, n - 1, n - 3))
    return t.reshape(a.shape[:-3] + (2 * C_DH, a.shape[-3]))


def _sel_flash_kernel(tq, tk, ii_ref, jj_ref, q_ref, kv_ref, msk_ref, o_ref, m_sc, l_sc, acc_sc):
    s_idx = pl.program_id(0)
    i = ii_ref[s_idx]
    j = jj_ref[s_idx]
    bpt = tk // CMP_BLOCK

    @pl.when(j == 0)
    def _():
        m_sc[...] = jnp.full(m_sc.shape, NEG_INF, F32)
        l_sc[...] = jnp.zeros(l_sc.shape, F32)
        acc_sc[...] = jnp.zeros(acc_sc.shape, F32)

    kv = kv_ref[...]
    kv_aug = jnp.concatenate([kv.astype(BF16), jnp.ones((tk, LANES), BF16)], axis=1)
    kk = _kk(kv)
    nbl = msk_ref.shape[1]
    boff = (j * bpt) % nbl
    expand = jnp.where(_row((nbl, tk)) == boff + _lane((nbl, tk)) // CMP_BLOCK, 1.0, 0.0).astype(BF16)
    sel = _mm(msk_ref[...].astype(BF16), expand) > 0.5

    def step(masked, nc=tk):
        valid = sel[:, 0:nc]
        if masked:
            valid = jnp.logical_and(valid, (j * tk + _lane((tq, nc))) <= (i * tq + _row((tq, nc))))
        for h in range(C_HEADS):
            pv, m_next, alpha = _flash_step(_nt(_head_qb(q_ref, h), kk[0:nc]), valid, m_sc[h], kv_aug[0:nc])
            acc_sc[h] = acc_sc[h] * alpha + pv[:, :128]
            l_sc[h] = l_sc[h] * alpha + pv[:, 128:]
            m_sc[h] = m_next

    _causal_dispatch(step, i, j, tq, tk)

    @pl.when(j == ((i + 1) * tq - 1) // tk)
    def _():
        for g in range(2):
            o_ref[:, 128 * g:128 * (g + 1)] = _pair_out(acc_sc[2 * g] / l_sc[2 * g],
                                                        acc_sc[2 * g + 1] / l_sc[2 * g + 1])


def _sel_flash(cqr, sel, selmask, tq, tk):
    L = cqr.shape[0]
    nb = selmask.shape[1]
    nbl = min(nb, LANES)
    bpt = tk // CMP_BLOCK
    ii, jj = _causal_pairs(L // tq, tq, tk)
    grid_spec = pltpu.PrefetchScalarGridSpec(
        num_scalar_prefetch=2,
        grid=(int(ii.shape[0]),),
        in_specs=[pl.BlockSpec((tq, 256), lambda s, ii, jj: (ii[s], 0)),
                  pl.BlockSpec((tk, 128), lambda s, ii, jj: (jj[s], 0)),
                  pl.BlockSpec((tq, nbl), lambda s, ii, jj: (ii[s], (jj[s] * bpt) // nbl))],
        out_specs=pl.BlockSpec((tq, 256), lambda s, ii, jj: (ii[s], 0)),
        scratch_shapes=[pltpu.VMEM((4, tq, LANES), F32), pltpu.VMEM((4, tq, LANES), F32),
                        pltpu.VMEM((4, tq, 128), F32)])
    return pl.pallas_call(
        functools.partial(_sel_flash_kernel, tq, tk),
        grid_spec=grid_spec,
        out_shape=jax.ShapeDtypeStruct((L, 256), F32),
        compiler_params=_cp(("arbitrary",)),
        name="sel_flash",
    )(ii, jj, cqr, sel, selmask)


def _win_prompt_kernel(t, q_ref, kp_ref, kc_ref, o_ref):
    i = pl.program_id(0)
    kvp = kp_ref[...]
    kvc = kc_ref[...]
    kkp, kkc = _kk(kvp), _kk(kvc)
    r, c = _row((t, t)), _lane((t, t))
    vp = jnp.logical_and(c > r, i > 0)
    vc = c <= r
    outs = []
    for h in range(C_HEADS):
        qh = _head_qb(q_ref, h)
        sp =jnp.where(vp, _nt(qh, kkp), NEG_INF)
        sc = jnp.where(vc, _nt(qh, kkc), NEG_INF)
        m = jnp.maximum(jnp.max(sp, axis=1, keepdims=True), jnp.max(sc, axis=1, keepdims=True))
        pp = jnp.exp2(sp - m)
        pc = jnp.exp2(sc - m)
        l = jnp.sum(pp, axis=1, keepdims=True) + jnp.sum(pc, axis=1, keepdims=True)
        outs.append((_mm(pp.astype(BF16), kvp.astype(BF16)) + _mm(pc.astype(BF16), kvc.astype(BF16))) / l)
    for g in range(2):
        o_ref[:, 128 * g:128 * (g + 1)] = _pair_out(outs[2 * g], outs[2 * g + 1])


def _win_prompt(cqr, win, t):
    L = cqr.shape[0]
    return pl.pallas_call(
        functools.partial(_win_prompt_kernel, t),
        grid=(L // t,),
        in_specs=[pl.BlockSpec((t, 256), lambda i: (i, 0)),
                  pl.BlockSpec((t, 128), lambda i: (jnp.maximum(i - 1, 0), 0)),
                  pl.BlockSpec((t, 128), lambda i: (i, 0))],
        out_specs=pl.BlockSpec((t, 256), lambda i: (i, 0)),
        out_shape=jax.ShapeDtypeStruct((L, 256), F32),
        compiler_params=_cp(("arbitrary",)),
        name="win_prompt",
    )(cqr, win, win)


def _cmp_kernel(tq, qpos0, topn, q_ref, kv_ref, o_ref, msk_ref):
    t = pl.program_id(1)
    bg = q_ref.shape[0]
    nb = kv_ref.shape[1]
    blk = _lane((tq, nb))
    qpos = qpos0 + t * tq + _row((tq, nb))
    valid = (blk * CMP_BLOCK + (CMP_BLOCK - 1)) <= qpos
    imps = []
    for b in range(bg):
        kv = kv_ref[b]
        kvb = kv.astype(BF16)
        kk = _kk(kv)
        q = q_ref[b]
        imp = jnp.zeros((tq, nb), F32)
        outs = []
        for h in range(C_HEADS):
            s = jnp.where(valid, _nt(_head_qb(q, h), kk), NEG_INF)
            m = jnp.max(s, axis=1, keepdims=True)
            p = jnp.where(valid, jnp.exp2(s - m), 0.0)
            p = p / jnp.maximum(jnp.sum(p, axis=1, keepdims=True), 1e-30)
            outs.append(_mm(p.astype(BF16), kvb))
            imp = imp + p
        for g in range(2):
            o_ref[b, :, 128 * g:128 * (g + 1)] = _pair_out(outs[2 * g], outs[2 * g + 1])
        imps.append(imp)
    rows = bg * tq
    imp = jnp.concatenate(imps, axis=0) if bg > 1 else imps[0]
    blk = _lane((rows, nb))
    cur = (qpos0 + t * tq + _row((rows, nb)) % tq) // CMP_BLOCK
    score = jnp.where(jnp.logical_or(blk == 0, blk == cur), FORCED_SCORE, jnp.where(blk < cur, imp, -1.0))
    chosen = jnp.zeros((rows, nb), F32)
    blkf = blk.astype(F32)
    for _ in range(topn):
        mx = jnp.max(score, axis=1, keepdims=True)
        idx = jnp.min(jnp.where(score == mx, blkf, float(nb)), axis=1, keepdims=True)
        hit = blkf == idx
        chosen = jnp.where(hit, 1.0, chosen)
        score = jnp.where(hit, -3.0e38, score)
    for b in range(bg):
        msk_ref[b] = chosen[b * tq:(b + 1) * tq]


def _cmp_attend(cq, kvcmp, qpos0, topn, tq, bg):
    B, L, _ = cq.shape
    nb = kvcmp.shape[1]
    return pl.pallas_call(
        functools.partial(_cmp_kernel, tq, qpos0, topn),
        grid=(B // bg, L // tq),
        in_specs=[pl.BlockSpec((bg, tq, 256), lambda b, t: (b, t, 0)),
                  pl.BlockSpec((bg, nb, 128), lambda b, t: (b, 0, 0))],
        out_specs=[pl.BlockSpec((bg, tq, 256), lambda b, t: (b, t, 0)),
                   pl.BlockSpec((bg, tq, nb), lambda b, t: (b, t, 0))],
        out_shape=[jax.ShapeDtypeStruct((B, L, 256), F32), jax.ShapeDtypeStruct((B, L, nb), F32)],
        compiler_params=_cp(("arbitrary", "arbitrary")),
        name="cmp_attend_select",
    )(cq, kvcmp)


def _cmp_weight(w_cmp):
    z = jnp.zeros((CMP_BLOCK, C_DH, C_DH), w_cmp.dtype)
    k_part = jnp.stack([jnp.concatenate([w_cmp[0], z], axis=-1), jnp.concatenate([z, w_cmp[1]], axis=-1)], axis=1)
    return k_part.reshape(CMP_BLOCK * 2 * C_DH, 2 * C_DH).astype(BF16)


def _summ_kernel(x_ref, w_ref, o_ref):
    o_ref[...] = _mm(x_ref[...].astype(BF16), w_ref[...])


def _summaries_prompt(cmp_rows, wc):
    L = cmp_rows.shape[0]
    nb = L // CMP_BLOCK
    kdim = CMP_BLOCK * 128
    tr = min(nb, 128)
    return pl.pallas_call(
        _summ_kernel,
        grid=(nb // tr,),
        in_specs=[pl.BlockSpec((tr, kdim), lambda i: (i, 0)), pl.BlockSpec((kdim, 128), lambda i: (0, 0))],
        out_specs=pl.BlockSpec((tr, 128), lambda i: (i, 0)),
        out_shape=jax.ShapeDtypeStruct((nb, 128), F32),
        compiler_params=_cp(("arbitrary",)),
        name="cmp_summaries_prompt",
    )(cmp_rows.reshape(nb, kdim), wc)


def _cmp_weight_by_feature(w_cmp, bpp):
    eye_b = jnp.eye(bpp, dtype=w_cmp.dtype)
    eye_c = jnp.eye(2, dtype=w_cmp.dtype)
    t = jnp.einsum("cjde,bB,cC->cdbjBCe", w_cmp, eye_b, eye_c)
    return t.reshape(2 * C_DH, bpp * CMP_BLOCK, bpp * 2 * C_DH).astype(BF16)


def _summ_paged_kernel(npg, layer, pt_ref, w_ref, pool_ref, o_ref, stage, sem):
    b = pl.program_id(0)
    slot = b % 2

    def page_copy(bb, p, sl):
        return pltpu.make_async_copy(pool_ref.at[layer, pt_ref[bb * npg + p]],
                                     stage.at[sl, :, pl.ds(p, 1), :], sem.at[sl])

    def start_all(bb, sl):
        lax.fori_loop(0, npg, lambda p, c: (page_copy(bb, p, sl).start(), c)[1], 0)

    def wait_all(bb, sl):
        lax.fori_loop(0, npg, lambda p, c: (page_copy(bb, p, sl).wait(), c)[1], 0)

    @pl.when(b == 0)
    def _():
        start_all(0, 0)

    @pl.when(b + 1 < pl.num_programs(0))
    def _():
        start_all(b + 1, 1 - slot)

    wait_all(b, slot)

    def body(cd, acc):
        return acc + _mm(stage[slot, cd].astype(BF16), w_ref[cd])

    o_ref[0] = lax.fori_loop(0, stage.shape[1], body, jnp.zeros(o_ref.shape[1:], F32), unroll=4)


def _summaries_paged(pool, layer, page_table, w_feat):
    depth, n_pool, page = pool.shape[:3]
    B, npg = page_table.shape
    nfeat = 2 * C_DH
    view = _feature_major(pool).reshape(depth, n_pool, nfeat, 1, page)
    wout = w_feat.shape[2]
    grid_spec = pltpu.PrefetchScalarGridSpec(
        num_scalar_prefetch=1, grid=(B,),
        in_specs=[pl.BlockSpec(w_feat.shape, lambda b, pt: (0, 0, 0)),
                  pl.BlockSpec(memory_space=pl.ANY)],
        out_specs=pl.BlockSpec((1, npg, wout), lambda b, pt: (b, 0, 0)),
        scratch_shapes=[pltpu.VMEM((2, nfeat, npg, page), F32), pltpu.SemaphoreType.DMA((2,))])
    out = pl.pallas_call(
        functools.partial(_summ_paged_kernel, npg, layer),
        grid_spec=grid_spec,
        out_shape=jax.ShapeDtypeStruct((B, npg, wout), F32),
        compiler_params=_cp(("arbitrary",), 56),
        name="cmp_summaries_paged",
    )(page_table.reshape(-1), w_feat, view)
    return out.reshape(B, npg * (page // CMP_BLOCK), 128)


def _hgrn_kernel(C, nchunk, q_ref, k_ref, v_ref, g_ref, s0_ref, o_ref, st_ref, st_sc):
    t = pl.program_id(1)

    @pl.when(t == 0)
    def _():
        st_sc[...] = s0_ref[0]

    W = B_HEADS * B_DK
    head_mask = (_row((W, W)) // B_DV) == (_lane((W, W)) // B_DK)
    ones_blk = jnp.where(head_mask, 1.0, 0.0).astype(BF16)
    tri = jnp.where(_row((C, C)) >= _lane((C, C)), 1.0, 0.0)
    srow = _row((C, W))

    def chunk(c, carry):
        r0 = pl.multiple_of(c * C, C)
        q = q_ref[0, pl.ds(r0, C), :]
        k = k_ref[0, pl.ds(r0, C), :]
        v = v_ref[0, pl.ds(r0, C), :]
        g = g_ref[0, pl.ds(r0, C), :]
        G = lax.dot_general(tri, g, (((1,), (0,)), ((), ())), preferred_element_type=F32,
                            precision=lax.Precision.HIGHEST)
        st = st_sc[...]
        ps = []
        for tt in range(C):
            d = jnp.where(srow <= tt, G[tt:tt + 1, :] - G, NEG_INF)
            ps.append(q[tt:tt + 1, :] * jnp.exp(d) * k)
        P = jnp.concatenate(ps, axis=0)
        R = _mm(P.astype(BF16), ones_blk)
        o_intra = jnp.sum(R.reshape(C, C, W) * v[None, :, :], axis=1)
        o_inter = _nt((q * jnp.exp(G)).astype(BF16), st.astype(BF16))
        o_ref[0, pl.ds(r0, C), :] = o_intra + o_inter
        g_last = G[C - 1:C, :]
        upd = _tn(v.astype(BF16), (k * jnp.exp(g_last - G)).astype(BF16))
        st_sc[...] = st * jnp.exp(g_last) + jnp.where(head_mask, upd, 0.0)
        return carry

    lax.fori_loop(0, nchunk, chunk, 0, unroll=min(nchunk, 8))

    @pl.when(t == pl.num_programs(1) - 1)
    def _():
        st_ref[0] = st_sc[...]


def _hgrn(hq, hk, hv, hf, s0t, C, tr):
    B, L, W = hq.shape
    seq = pl.BlockSpec((1, tr, W), lambda b, t: (b, t, 0))
    st = pl.BlockSpec((1, W, W), lambda b, t: (b, 0, 0))
    return pl.pallas_call(
        functools.partial(_hgrn_kernel, C, tr // C),
        grid=(B, L // tr),
        in_specs=[seq, seq, seq, seq, st],
        out_specs=[seq, st],
        out_shape=[jax.ShapeDtypeStruct((B, L, W), F32), jax.ShapeDtypeStruct((B, W, W), F32)],
        scratch_shapes=[pltpu.VMEM((W, W), F32)],
        compiler_params=_cp(("arbitrary", "arbitrary")),
        name="hgrn2",
    )(hq, hk, hv, hf, s0t)


def _state_to_blockdiag_t(s):
    B = s.shape[0]
    eye = jnp.eye(B_HEADS, dtype=s.dtype)
    return jnp.einsum("bhkv,hg->bhvgk", s, eye).reshape(B, B_HEADS * B_DV, B_HEADS * B_DK)


def _blockdiag_t_to_state(st):
    B = st.shape[0]
    st5 = st.reshape(B, B_HEADS, B_DV, B_HEADS, B_DK)
    return jnp.stack([st5[:, h, :, h, :] for h in range(B_HEADS)], axis=1).transpose(0, 1, 3, 2)


def _diff_dec_kernel(G, page, lam_init, pt_ref, q_ref, kn_ref, vn_ref, dl_ref, *rest):
    kps, vps = rest[:G], rest[G:2 * G]
    o_ref = rest[2 * G]
    m_sc, l_sc, acc_sc = rest[2 * G + 1:]
    s_idx = pl.program_id(1)
    nq = q_ref.shape[1]

    @pl.when(s_idx == 0)
    def _():
        m_sc[...] = jnp.full(m_sc.shape, NEG_INF, F32)
        l_sc[...] = jnp.zeros(l_sc.shape, F32)
        acc_sc[...] = jnp.zeros(acc_sc.shape, F32)

    qparts = []
    for h in range(A_HEADS):
        qp = q_ref[0, :, 128 * h:128 * (h + 1)]
        lane = _lane(qp.shape)
        qparts += [jnp.where(lane < 64, qp, 0.0), jnp.where(lane >= 64, qp, 0.0)]
    qall = jnp.concatenate(qparts, axis=0).astype(BF16)
    rows = qall.shape[0]
    rpp = page * A_HEADS
    same_head = (_lane((rows, G * rpp)) % A_HEADS) == (_row((rows, G * rpp)) // (2 * nq))
    kcat = jnp.concatenate([kps[g][...].astype(BF16) for g in range(G)], axis=0)
    vcat = jnp.concatenate([vps[g][...].astype(BF16) for g in range(G)], axis=0)
    s = jnp.where(same_head, _nt(qall, kcat), NEG_INF)
    p, alpha, m_next, l_next = _online(s, m_sc[:, :1], l_sc[:, :1])
    pv = _mm(p.astype(BF16), vcat)
    acc_sc[...] = acc_sc[...] * alpha + pv
    m_sc[...] = jnp.broadcast_to(m_next, m_sc.shape)
    l_sc[...] = jnp.broadcast_to(l_next, l_sc.shape)

    @pl.when(s_idx == pl.num_programs(1) - 1)
    def _():
        lam = _diff_lambda(dl_ref[...], lam_init)
        kn = jnp.concatenate([kn_ref[0, :, 128 * h:128 * (h + 1)] for h in range(A_HEADS)], axis=0)
        vn = jnp.concatenate([vn_ref[0, :, 128 * h:128 * (h + 1)] for h in range(A_HEADS)], axis=0)
        nk = A_HEADS * nq
        r, c = _row((rows, nk)), _lane((rows, nk))
        ok = jnp.logical_and(c // nq == r // (2 * nq), c % nq <= r % nq)
        sn = jnp.where(ok, _nt(qall, kn.astype(BF16)), NEG_INF)
        p2, alpha2, m2, l2 = _online(sn, m_sc[:, :1], l_sc[:, :1])
        o = (acc_sc[...] * alpha2 + _mm(p2.astype(BF16), vn.astype(BF16))) / l2
        for h in range(A_HEADS):
            o_ref[0, :, 128 * h:128 * (h + 1)] = (o[2 * nq * h:2 * nq * h + nq]
                                                  - lam * o[2 * nq * h + nq:2 * nq * (h + 1)])


def _diff_decode(aq, ak_new, av_new, dl, lam_init, pool_k, pool_v, layer, page_table, G):
    B, nq, _ = aq.shape
    depth, n_pool, page = pool_k.shape[:3]
    npg = page_table.shape[1]
    rpp = page * A_HEADS
    kview = pool_k.reshape(depth, n_pool, rpp, 128)
    vview = pool_v.reshape(depth, n_pool, rpp, 128)
    pt = page_table.reshape(-1)
    rows = A_HEADS * 2 * nq

    def page_spec(g):
        return pl.BlockSpec((None, None, rpp, 128), lambda b, s, pt: (layer, pt[b * npg + s * G + g], 0, 0))

    tok = pl.BlockSpec((1, nq, 512), lambda b, s, pt: (b, 0, 0))
    grid_spec = pltpu.PrefetchScalarGridSpec(
        num_scalar_prefetch=1, grid=(B, npg // G),
        in_specs=[tok, tok, tok, pl.BlockSpec(dl.shape, lambda b, s, pt: (0, 0))]
        + [page_spec(g) for g in range(G)] * 2,
        out_specs=tok,
        scratch_shapes=[pltpu.VMEM((rows, LANES), F32), pltpu.VMEM((rows, LANES), F32),
                        pltpu.VMEM((rows, 128), F32)])
    return pl.pallas_call(
        functools.partial(_diff_dec_kernel, G, page, lam_init),
        grid_spec=grid_spec,
        out_shape=jax.ShapeDtypeStruct((B, nq, 512), F32),
        compiler_params=_cp(("arbitrary", "arbitrary")),
        name="diff_decode",
    )(pt, aq, ak_new, av_new, dl, *([kview] * G), *([vview] * G))


def _sel_dec_kernel(G, page, pt_ref, q_ref, new_ref, msk_ref, *rest):
    pages = rest[:G]
    o_ref = rest[G]
    m_sc, l_sc, acc_sc = rest[G + 1:]
    s_idx = pl.program_id(1)
    nq = q_ref.shape[1]
    rows = C_HEADS * nq
    bpp = page // CMP_BLOCK

    @pl.when(s_idx == 0)
    def _():
        m_sc[...] = jnp.full(m_sc.shape, NEG_INF, F32)
        l_sc[...] = jnp.zeros(l_sc.shape, F32)
        acc_sc[...] = jnp.zeros(acc_sc.shape, F32)

    qall = _stack_low_q(q_ref[0])
    nb = msk_ref.shape[2]
    nkeys = G * page
    msk = msk_ref[0]
    msk4 = jnp.concatenate([msk] * C_HEADS, axis=0).astype(BF16)
    expand = jnp.where(_row((nb, nkeys)) == s_idx * (G * bpp) + _lane((nb, nkeys)) // CMP_BLOCK, 1.0, 0.0)
    valid = _mm(msk4, expand.astype(BF16)) > 0.5
    kvcat = jnp.concatenate([pages[g][...].astype(BF16) for g in range(G)], axis=1)
    s = jnp.where(valid, _mm(qall, kvcat), NEG_INF)
    p, alpha, m_next, l_next = _online(s, m_sc[:, :1], l_sc[:, :1])
    pv = _nt(p.astype(BF16), kvcat)
    acc_sc[...] = acc_sc[...] * alpha + pv
    m_sc[...] = jnp.broadcast_to(m_next, m_sc.shape)
    l_sc[...] = jnp.broadcast_to(l_next, l_sc.shape)

    @pl.when(s_idx == pl.num_programs(1) - 1)
    def _():
        kvn = new_ref[0].astype(BF16)
        causal = _lane((rows, nq)) <= (_row((rows, nq)) % nq)
        sn = jnp.where(causal, _nt(qall, kvn), NEG_INF)
        p2, alpha2, m2, l2 = _online(sn, m_sc[:, :1], l_sc[:, :1])
        o = (acc_sc[...] * alpha2 + _mm(p2.astype(BF16), kvn)) / l2
        for g in range(2):
            o_ref[0, :, 128 * g:128 * (g + 1)] = _pair_out(o[2 * g * nq:(2 * g + 1) * nq],
                                                           o[(2 * g + 1) * nq:(2 * g + 2) * nq])


def _sel_decode(cqr, sel_new, selmask, pool, layer, page_table, G):
    B, nq, _ = cqr.shape
    depth, n_pool, page = pool.shape[:3]
    npg = page_table.shape[1]
    nb = selmask.shape[2]
    view = _feature_major(pool)
    pt = page_table.reshape(-1)
    rows = C_HEADS * nq

    def page_spec(g):
        return pl.BlockSpec((None, None, 128, page), lambda b, s, pt: (layer, pt[b * npg + s * G + g], 0, 0))

    grid_spec = pltpu.PrefetchScalarGridSpec(
        num_scalar_prefetch=1, grid=(B, npg // G),
        in_specs=[pl.BlockSpec((1, nq, 256), lambda b, s, pt: (b, 0, 0)),
                  pl.BlockSpec((1, nq, 128), lambda b, s, pt: (b, 0, 0)),
                  pl.BlockSpec((1, nq, nb), lambda b, s, pt: (b, 0, 0))] + [page_spec(g) for g in range(G)],
        out_specs=pl.BlockSpec((1, nq, 256), lambda b, s, pt: (b, 0, 0)),
        scratch_shapes=[pltpu.VMEM((rows, LANES), F32), pltpu.VMEM((rows, LANES), F32),
                        pltpu.VMEM((rows, 128), F32)])
    return pl.pallas_call(
        functools.partial(_sel_dec_kernel, G, page),
        grid_spec=grid_spec,
        out_shape=jax.ShapeDtypeStruct((B, nq, 256), F32),
        compiler_params=_cp(("arbitrary", "arbitrary")),
        name="sel_decode",
    )(pt, cqr, sel_new, selmask, *([view] * G))


def _win_dec_kernel(q_ref, buf_ref, new_ref, o_ref):
    nq = q_ref.shape[1]
    W = buf_ref.shape[2]
    rows = C_HEADS * nq
    qall = _stack_low_q(q_ref[0])
    kvb = buf_ref[0].astype(BF16)
    kvn = new_ref[0].astype(BF16)
    tok = _row((rows, W)) % nq
    sb = jnp.where(_lane((rows, W)) > tok, _mm(qall, kvb), NEG_INF)
    sn = jnp.where(_lane((rows, nq)) <= _row((rows, nq)) % nq, _nt(qall, kvn), NEG_INF)
    m = jnp.maximum(jnp.max(sb, axis=1, keepdims=True), jnp.max(sn, axis=1, keepdims=True))
    pb, pn = jnp.exp2(sb - m), jnp.exp2(sn - m)
    l = jnp.sum(pb, axis=1, keepdims=True) + jnp.sum(pn, axis=1, keepdims=True)
    o = (_nt(pb.astype(BF16), kvb) + _mm(pn.astype(BF16), kvn)) / l
    for g in range(2):
        o_ref[0, :, 128 * g:128 * (g + 1)] = _pair_out(o[2 * g * nq:(2 * g + 1) * nq],
                                                       o[(2 * g + 1) * nq:(2 * g + 2) * nq])


def _win_decode(cqr, win_buf, win_new):
    B, nq, _ = cqr.shape
    W = win_buf.shape[2]
    return pl.pallas_call(
        _win_dec_kernel,
        grid=(B,),
        in_specs=[pl.BlockSpec((1, nq, 256), lambda b: (b, 0, 0)),
                  pl.BlockSpec((1, 128, W), lambda b: (b, 0, 0)),
                  pl.BlockSpec((1, nq, 128), lambda b: (b, 0, 0))],
        out_specs=pl.BlockSpec((1, nq, 256), lambda b: (b, 0, 0)),
        out_shape=jax.ShapeDtypeStruct((B, nq, 256), F32),
        compiler_params=_cp(("arbitrary",)),
        name="win_decode",
    )(cqr, win_buf, win_new)


def _merge_kernel(lam_init, x_ref, g1_ref, ao_ref, bo_ref, hg_ref, oc_ref, os_ref, ow_ref, cg_ref,
                  sub_ref, hn_ref, w_ref, o_ref):
    tm = x_ref.shape[0]
    sub = sub_ref[...]
    y = jnp.zeros((tm, w_ref.shape[1]), F32)
    for h in range(A_HEADS):
        a = ao_ref[:, 128 * h:128 * (h + 1)]
        a = a * lax.rsqrt(jnp.mean(a * a, axis=-1, keepdims=True) + RMS_EPS) * sub * (1.0 - lam_init)
        y = y + _mm(a.astype(BF16), w_ref[128 * h:128 * (h + 1), :])
    lane = _lane((tm, 128))
    lo = lane < 64
    hn = hn_ref[...]
    for g in range(2):
        b = bo_ref[:, 128 * g:128 * (g + 1)]
        b2 = b * b
        s_lo = jnp.sum(jnp.where(lo, b2, 0.0), axis=-1, keepdims=True)
        s_hi = jnp.sum(jnp.where(lo, 0.0, b2), axis=-1, keepdims=True)
        ms = jnp.where(lo, s_lo, s_hi) * (1.0 / B_DV)
        b = b * lax.rsqrt(ms + RMS_EPS) * hn * _silu(hg_ref[:, 128 * g:128 * (g + 1)])
        y = y + _mm(b.astype(BF16), w_ref[512 + 128 * g:512 + 128 * (g + 1), :])
    sig = jax.nn.sigmoid(cg_ref[...])
    for g in range(2):
        c = jnp.zeros((tm, 128), F32)
        for br, ref in enumerate((oc_ref, os_ref, ow_ref)):
            ge = jnp.broadcast_to(sig[:, 6 * g + br:6 * g + br + 1], (tm, 128))
            go = jnp.broadcast_to(sig[:, 6 * g + 3 + br:6 * g + 3 + br + 1], (tm, 128))
            c = c + jnp.where(lo, ge, go) * ref[:, 128 * g:128 * (g + 1)]
        y = y + _mm(c.astype(BF16), w_ref[768 + 128 * g:768 + 128 * (g + 1), :])
    o_ref[...] = x_ref[...] + g1_ref[...] * y


def _merge(lam_init, x, g1, a_o, b_o, hg, o_cmp, o_sel, o_win, cg, sub, hn2, w_bf, tm):
    rows, d = x.shape
    mod_spec = (pl.BlockSpec((1, d), lambda i: (0, 0)) if g1.shape[0] == 1
                else pl.BlockSpec((tm, d), lambda i: (i, 0)))

    def rs(w):
        return pl.BlockSpec((tm, w), lambda i: (i, 0))

    def full(a):
        return pl.BlockSpec(a.shape, lambda i: (0, 0))

    return pl.pallas_call(
        functools.partial(_merge_kernel, lam_init),
        grid=(rows // tm,),
        in_specs=[rs(d), mod_spec, rs(512), rs(256), rs(256), rs(256), rs(256), rs(256), rs(12),
                  full(sub), full(hn2), full(w_bf)],
        out_specs=rs(d),
        out_shape=jax.ShapeDtypeStruct((rows, d), F32),
        compiler_params=_cp(("arbitrary",)),
        name="merge_out_proj",
    )(x, g1, a_o, b_o, hg, o_cmp, o_sel, o_win, cg, sub, hn2, w_bf)


def _ffn_kernel(fc, x_ref, sc_ref, sh_ref, g2_ref, nw_ref, wgu_ref, wd_ref, o_ref):
    x = x_ref[...]
    hb = _rms_mod(x, nw_ref[...], sc_ref[...], sh_ref[...]).astype(BF16)
    dff = wd_ref.shape[0]
    y = jnp.zeros(x.shape, F32)
    for c in range(dff // fc):
        g = _mm(hb, wgu_ref[:, c * fc:(c + 1) * fc])
        u = _mm(hb, wgu_ref[:, dff + c * fc:dff + (c + 1) * fc])
        y = y + _mm((_silu(g) * u).astype(BF16), wd_ref[c * fc:(c + 1) * fc, :])
    o_ref[...] = x + g2_ref[...] * y


def _ffn_dense(x, sc, sh, g2, nw, wgu_bf, wd_bf, tm, fc):
    rows, d = x.shape
    mod_spec = (pl.BlockSpec((1, d), lambda i: (0, 0)) if sc.shape[0] == 1
                else pl.BlockSpec((tm, d), lambda i: (i, 0)))
    return pl.pallas_call(
        functools.partial(_ffn_kernel, fc),
        grid=(rows // tm,),
        in_specs=[pl.BlockSpec((tm, d), lambda i: (i, 0)), mod_spec, mod_spec, mod_spec,
                  pl.BlockSpec((1, d), lambda i: (0, 0)),
                  pl.BlockSpec(wgu_bf.shape, lambda i: (0, 0)),
                  pl.BlockSpec(wd_bf.shape, lambda i: (0, 0))],
        out_specs=pl.BlockSpec((tm, d), lambda i: (i, 0)),
        out_shape=jax.ShapeDtypeStruct((rows, d), F32),
        compiler_params=_cp(("arbitrary",), 56),
        name="ffn_dense",
    )(x, sc, sh, g2, nw, wgu_bf, wd_bf)


def _router_kernel(x_ref, sc_ref, sh_ref, nw_ref, wr_ref, br_ref, gate_ref, pos_ref, cnt_ref):
    h = _rms_mod(x_ref[...], nw_ref[...], sc_ref[...], sh_ref[...])
    tm = h.shape[0]
    logits = lax.dot_general(wr_ref[...], h, (((1,), (1,)), ((), ())), preferred_element_type=F32,
                             precision=lax.Precision.HIGHEST) + br_ref[...]
    ne = logits.shape[0]
    row = _row((ne, tm)).astype(F32)
    m1 = jnp.max(logits, axis=0, keepdims=True)
    i1 = jnp.min(jnp.where(logits == m1, row, float(ne)), axis=0, keepdims=True)
    hit1 = row == i1
    l2 = jnp.where(hit1, -3.0e38, logits)
    m2 = jnp.max(l2, axis=0, keepdims=True)
    i2 = jnp.min(jnp.where(l2 == m2, row, float(ne)), axis=0, keepdims=True)
    hit2 = row == i2
    e2 = jnp.exp(m2 - m1)
    den = 1.0 + e2
    gate_ref[...] = jnp.where(hit1, 1.0 / den, jnp.where(hit2, e2 / den, 0.0))
    member = jnp.logical_or(hit1, hit2)
    mem = jnp.where(member, 1.0, 0.0)
    before = jnp.where(_row((tm, tm)) < _lane((tm, tm)), 1.0, 0.0).astype(BF16)
    pos = _mm(mem.astype(BF16), before)
    pos_ref[...] = jnp.where(member, pos, -1.0)
    cnt_ref[0] = jnp.broadcast_to(jnp.sum(mem, axis=1, keepdims=True), (ne, LANES)).astype(jnp.int32)


def _router(x, sc, sh, nw, w_router, b_router, tm):
    rows, d = x.shape
    ne = w_router.shape[1]
    nt = rows // tm
    mod_spec = (pl.BlockSpec((1, d), lambda i: (0, 0)) if sc.shape[0] == 1
                else pl.BlockSpec((tm, d), lambda i: (i, 0)))
    return pl.pallas_call(
        _router_kernel,
        grid=(nt,),
        in_specs=[pl.BlockSpec((tm, d), lambda i: (i, 0)), mod_spec, mod_spec,
                  pl.BlockSpec((1, d), lambda i: (0, 0)),
                  pl.BlockSpec((ne, d), lambda i: (0, 0)),
                  pl.BlockSpec((ne, 1), lambda i: (0, 0))],
        out_specs=[pl.BlockSpec((ne, tm), lambda i: (0, i)), pl.BlockSpec((ne, tm), lambda i: (0, i)),
                   pl.BlockSpec((1, ne, LANES), lambda i: (i, 0, 0))],
        out_shape=[jax.ShapeDtypeStruct((ne, rows), F32), jax.ShapeDtypeStruct((ne, rows), F32),
                   jax.ShapeDtypeStruct((nt, ne, LANES), jnp.int32)],
        compiler_params=_cp(("arbitrary",)),
        name="moe_router",
    )(x, sc, sh, nw, w_router.T, b_router.reshape(ne, 1))


def _moe_kernel(nf, final_norm, cnt_ref, x_ref, sc_ref, sh_ref, g2_ref, nw_ref, nfin_ref, gate_ref, pos_ref,
                wg_ref, wu_ref, wd_ref, o_ref, hb_sc, acc_sc, xs_sc, ys_sc):
    i, e, f = pl.program_id(0), pl.program_id(1), pl.program_id(2)
    ne = pl.num_programs(1)
    tm = x_ref.shape[0]
    nsub = (cnt_ref[i * ne + e] + (MOE_SUB - 1)) // MOE_SUB

    @pl.when(jnp.logical_and(e == 0, f == 0))
    def _():
        hb_sc[...] = _rms_mod(x_ref[...], nw_ref[...], sc_ref[...], sh_ref[...]).astype(BF16)
        acc_sc[...] = jnp.zeros(acc_sc.shape, F32)

    def process(r0, nrows):
        rows = pl.ds(r0, nrows)

        def onehot():
            pos_row = pos_ref[pl.ds(e, 1), :]
            return pos_row == (_row((nrows, tm)) + r0).astype(F32)

        @pl.when(f == 0)
        def _():
            sel = jnp.where(onehot(), 1.0, 0.0).astype(BF16)
            xs_sc[rows, :] = _mm(sel, hb_sc[...]).astype(BF16)
            ys_sc[rows, :] = jnp.zeros((nrows, ys_sc.shape[1]), F32)

        xs = xs_sc[rows, :]
        act = (_silu(_mm(xs, wg_ref[0])) * _mm(xs, wu_ref[0])).astype(BF16)
        ys_sc[rows, :] += _mm(act, wd_ref[0])

        @pl.when(f == nf - 1)
        def _():
            oh = onehot()
            gate_row = gate_ref[pl.ds(e, 1), :]
            gr = jnp.sum(jnp.where(oh, gate_row, 0.0), axis=1, keepdims=True)
            y = ys_sc[rows, :] * gr
            y_hi = y.astype(BF16)
            y_lo = (y - y_hi.astype(F32)).astype(BF16)
            sel = jnp.where(oh, 1.0, 0.0).astype(BF16)
            acc_sc[...] += _tn(sel, y_hi) + _tn(sel, y_lo)

    cap = min(MOE_CAP, tm // MOE_SUB)
    pl.when(nsub > 0)(lambda: process(0, cap * MOE_SUB))

    def body(j, carry):
        process(pl.multiple_of(j * MOE_SUB, MOE_SUB), MOE_SUB)
        return carry

    lax.fori_loop(cap, jnp.maximum(nsub, cap), body, 0)

    @pl.when(jnp.logical_and(e == ne - 1, f == nf - 1))
    def _():
        xo = x_ref[...] + g2_ref[...] * acc_sc[...]
        if final_norm:
            xo = xo * lax.rsqrt(jnp.mean(xo * xo, axis=-1, keepdims=True) + RMS_EPS) * nfin_ref[...]
        o_ref[...] = xo


def _moe(x, sc, sh, g2, nw, nfin, gate_t, pos_t, counts, wgu_bf, wd_bf, tm, nf, final_norm):
    rows, d = x.shape
    ne, _, dff2 = wgu_bf.shape
    dff = dff2 // 2
    fc = dff // nf
    nt = rows // tm
    mod_spec = (pl.BlockSpec((1, d), lambda i, e, f, c: (0, 0)) if sc.shape[0] == 1
                else pl.BlockSpec((tm, d), lambda i, e, f, c: (i, 0)))
    vec = pl.BlockSpec((1, d), lambda i, e, f, c: (0, 0))
    route = pl.BlockSpec((ne, tm), lambda i, e, f, c: (0, i))
    grid_spec = pltpu.PrefetchScalarGridSpec(
        num_scalar_prefetch=1, grid=(nt, ne, nf),
        in_specs=[pl.BlockSpec((tm, d), lambda i, e, f, c: (i, 0)), mod_spec, mod_spec, mod_spec, vec, vec,
                  route, route,
                  pl.BlockSpec((1, d, fc), lambda i, e, f, c: (e, 0, f)),
                  pl.BlockSpec((1, d, fc), lambda i, e, f, c: (e, 0, nf + f)),
                  pl.BlockSpec((1, fc, d), lambda i, e, f, c: (e, f, 0))],
        out_specs=pl.BlockSpec((tm, d), lambda i, e, f, c: (i, 0)),
        scratch_shapes=[pltpu.VMEM((tm, d), BF16), pltpu.VMEM((tm, d), F32),
                        pltpu.VMEM((tm, d), BF16), pltpu.VMEM((tm, d), F32)])
    return pl.pallas_call(
        functools.partial(_moe_kernel, nf, final_norm),
        grid_spec=grid_spec,
        out_shape=jax.ShapeDtypeStruct((rows, d), F32),
        compiler_params=_cp(("arbitrary", "arbitrary", "arbitrary"), 56),
        name="moe_experts",
    )(counts, x, sc, sh, g2, nw, nfin, gate_t, pos_t, wgu_bf, wgu_bf, wd_bf)


def _norm_kernel(x_ref, w_ref, o_ref):
    x = x_ref[...]
    o_ref[...] = x * lax.rsqrt(jnp.mean(x * x, axis=-1, keepdims=True) + RMS_EPS) * w_ref[...]


def _final_norm(x, w, tm):
    rows, d = x.shape
    return pl.pallas_call(
        _norm_kernel, grid=(rows // tm,),
        in_specs=[pl.BlockSpec((tm, d), lambda i: (i, 0)), pl.BlockSpec((1, d), lambda i: (0, 0))],
        out_specs=pl.BlockSpec((tm, d), lambda i: (i, 0)),
        out_shape=jax.ShapeDtypeStruct((rows, d), F32),
        compiler_params=_cp(("arbitrary",)),
        name="final_norm",
    )(x, w)


def _rope_tables(pos):
    inv = ROPE_THETA ** (-jnp.arange(0, 64, 2, dtype=F32) / 64)
    ang = pos.astype(F32)[:, None] * inv[None, :]
    cos, sin = jnp.cos(ang), jnp.sin(ang)
    return jnp.tile(cos, (1, 4)), jnp.tile(jnp.concatenate([-sin, sin], axis=1), (1, 2))


def _pick(n, pref):
    t = min(n, pref)
    while n % t:
        t //= 2
    return t


def kernel(x_prompt, x_sample, cache_diff_k, cache_diff_v, cache_nsa_cmp, cache_nsa_sel, state_nsa_win, state_hgrn, page_table, c_prompt, c_sample, w_in, w_out, w_cmp, diff_lambda, diff_subln, hgrn_lb_logits, hgrn_norm, norm_mix, norm_ffn, norm_final, w_ada, b_ada, ffn_w_gate_up, ffn_w_down, moe_w_router, moe_b_router, moe_w_gate_up, moe_w_down):
    depth = w_in.shape[0]
    _, L, D = x_prompt.shape
    Bs, Ls, _ = x_sample.shape
    page = cache_diff_k.shape[2]
    past = page_table.shape[1] * page
    rows_s = Bs * Ls

    kvc_sample = [_summaries_paged(cache_nsa_cmp, l, page_table,
                                   _cmp_weight_by_feature(w_cmp[l], page // CMP_BLOCK)) for l in range(depth)]

    n_c = 1 + Bs
    n_cp = -(-n_c // 8) * 8
    c_all = jnp.concatenate([c_prompt, c_sample, jnp.zeros((n_cp - n_c, D), F32)], axis=0)
    mod = _ada(c_all, w_ada, b_ada)

    cos_p, sin_p = _rope_tables(jnp.arange(L))
    cos_s, sin_s = _rope_tables(past + (jnp.arange(rows_s) % Ls))

    xp = x_prompt.reshape(L, D)
    xs = x_sample.reshape(rows_s, D)
    tmp = _pick(L, 512)
    tq_a, tk_a = _pick(L, 512), _pick(L, 1024)
    tq_s, tk_s = _pick(L, 512), _pick(L, 1024)
    G_diff = _pick(page_table.shape[1], 16)
    G_sel = _pick(page_table.shape[1], 32)
    outs_p = [[] for _ in range(6)]
    outs_s = [[] for _ in range(6)]

    for l in range(depth):
        lam_init = 0.8 - 0.6 * math.exp(-0.3 * l)
        w_in_bf = w_in[l].astype(BF16)
        w_out_bf = w_out[l].astype(BF16)
        wc = _cmp_weight(w_cmp[l])
        dl = diff_lambda[l]
        sub = diff_subln[l].reshape(1, A_DV)
        hn2 = jnp.tile(hgrn_norm[l].reshape(1, B_DV), (1, 2))
        nmix = norm_mix[l].reshape(1, D)
        nffn = norm_ffn[l].reshape(1, D)
        mp = [m for m in jnp.split(mod[l, 0:1], 6, axis=-1)]
        ms = [jnp.repeat(m, Ls, axis=0) for m in jnp.split(mod[l, 1:1 + Bs], 6, axis=-1)]

        (aq, ak, av, hq, hk, hf, hv, hg, cq, cqr, cmpr, sel, win, cg) = _proj(
            l, xp, mp[1], mp[0], nmix, w_in_bf, cos_p, sin_p, hgrn_lb_logits, tmp)
        a_o = _diff_flash(aq, ak, av, dl, lam_init, tq_a, tk_a, A_HEADS)
        s0 = jnp.zeros((1, B_HEADS * B_DV, B_HEADS * B_DK), F32)
        b_o, st = _hgrn(hq[None], hk[None], hv[None], hf[None], s0, 16, _pick(L, 512))
        kvc = _summaries_prompt(cmpr, wc)
        o_cmp, selmask = _cmp_attend(cq[None], kvc[None], 0, TOP_N, _pick(L, 1024), 1)
        o_sel = _sel_flash(cqr, sel, selmask[0], tq_s, tk_s)
        o_win = _win_prompt(cqr, win, _pick(L, WINDOW))
        xp = _merge(lam_init, xp, mp[2], a_o, b_o[0], hg, o_cmp[0], o_sel, o_win, cg, sub, hn2, w_out_bf, tmp)
        W = min(WINDOW, L)
        for i, a in enumerate((ak.reshape(1, L, A_HEADS, 2 * A_DK), av.reshape(1, L, A_HEADS, A_DV),
                               cmpr.reshape(1, L, 2, C_DH), sel.reshape(1, L, 2, C_DH),
                               win[L - W:].reshape(1, W, 2, C_DH), _blockdiag_t_to_state(st))):
            outs_p[i].append(a)

        (aq, ak, av, hq, hk, hf, hv, hg, cq, cqr, cmpr, sel, win, cg) = _proj(
            l, xs, ms[1], ms[0], nmix, w_in_bf, cos_s, sin_s, hgrn_lb_logits, rows_s)
        b3 = lambda a: a.reshape(Bs, Ls, a.shape[-1])
        a_o = _diff_decode(b3(aq).astype(F32), b3(ak), b3(av), dl, lam_init, cache_diff_k, cache_diff_v, l,
                           page_table, G_diff)
        b_o, st = _hgrn(b3(hq), b3(hk), b3(hv), b3(hf), _state_to_blockdiag_t(state_hgrn[l]), Ls, Ls)
        o_cmp, selmask = _cmp_attend(b3(cq).astype(F32), kvc_sample[l], past, TOP_N - 1, Ls, _pick(Bs, 16))
        o_sel = _sel_decode(b3(cqr).astype(F32), b3(sel), selmask, cache_nsa_sel, l, page_table, G_sel)
        o_win = _win_decode(b3(cqr).astype(F32), _feature_major(state_nsa_win[l]), b3(win))
        xs = _merge(lam_init, xs, ms[2], a_o.reshape(rows_s, 512), b_o.reshape(rows_s, 256), hg,
                    o_cmp.reshape(rows_s, 256), o_sel.reshape(rows_s, 256), o_win.reshape(rows_s, 256), cg,
                    sub, hn2, w_out_bf, rows_s)
        win_all = jnp.concatenate([state_nsa_win[l], win.reshape(Bs, Ls, 2, C_DH)], axis=1)[:, Ls:]
        for i, a in enumerate((ak.reshape(Bs, Ls, A_HEADS, 2 * A_DK), av.reshape(Bs, Ls, A_HEADS, A_DV),
                               cmpr.reshape(Bs, Ls, 2, C_DH), sel.reshape(Bs, Ls, 2, C_DH),
                               win_all, _blockdiag_t_to_state(st))):
            outs_s[i].append(a)

        j = l // 2
        last = l == depth - 1
        nfin = norm_final.reshape(1, D)
        if l % 2 == 0:
            wgu = ffn_w_gate_up[j].astype(BF16)
            wd = ffn_w_down[j].astype(BF16)
            fc = wd.shape[0] // 2
            xp = _ffn_dense(xp, mp[4], mp[3], mp[5], nffn, wgu, wd, tmp, fc)
            xs = _ffn_dense(xs, ms[4], ms[3], ms[5], nffn, wgu, wd, rows_s, fc)
            if last:
                xp = _final_norm(xp, nfin, tmp)
                xs = _final_norm(xs, nfin, rows_s)
        else:
            wgu = moe_w_gate_up[j].astype(BF16)
            wd = moe_w_down[j].astype(BF16)
            for which in range(2):
                x, m6, tm = (xp, mp, _pick(L, 1024)) if which == 0 else (xs, ms, rows_s)
                gate_t, pos_t, cnt = _router(x, m6[4], m6[3], nffn, moe_w_router[j], moe_b_router[j], tm)
                y = _moe(x, m6[4], m6[3], m6[5], nffn, nfin, gate_t, pos_t, cnt[:, :, 0].reshape(-1),
                         wgu, wd, tm, 2, last)
                if which == 0:
                    xp = y
                else:
                    xs = y
            if not last:
                pass

    y_prompt = xp.reshape(1, L, D)
    y_sample = xs.reshape(Bs, Ls, D)
    p_st = [jnp.stack(a, axis=0) for a in outs_p]
    s_st = [jnp.stack(a, axis=0) for a in outs_s]
    return (y_prompt, y_sample, *p_st, *s_st)
```
